```python
import math, functools
import jax, jax.numpy as jnp
from jax import lax
import numpy as np

D_MODEL = 1024
BATCH = 8
SEQ = 4096
DEPTH = 1

CTX_LEN = 256
GRID_W = 64
D_MIX = 2 * D_MODEL
HG_WIDTH = D_MODEL // 1
HG_HEADS = 8
HG_F = 128
HG_V = HG_WIDTH // HG_HEADS
HG_F_WIDTH = HG_HEADS * HG_F
HG_CHUNK = 64
SSD_WIDTH = D_MIX - HG_WIDTH
SSD_HEADDIM = 64
SSD_HEADS = SSD_WIDTH // SSD_HEADDIM
SSD_GROUPS = 4
SSD_HPG = SSD_HEADS // SSD_GROUPS
SSD_STATE = 128
SSD_CONV = 5
SSD_CHUNK = 128
SSD_CONV_CH = SSD_WIDTH + 2 * SSD_GROUPS * SSD_STATE
D_FF = ((8 * D_MODEL // 3 + 255) // 256) * 256
N_MOD = 6
EPS = 1e-6
IN_SPLITS = (HG_F_WIDTH, 2 * HG_F_WIDTH, HG_WIDTH, HG_WIDTH,
             SSD_WIDTH, SSD_CONV_CH, 2 * SSD_HEADS)
D_IN = sum(IN_SPLITS)

kernel_name = "hgrn2_ssd_parallel_heads_dit_layer"


def split_cols(p, sizes):
    idx = np.cumsum(sizes)[:-1].tolist()
    return jnp.split(p, idx, axis=-1)


def rms_norm(x, w):
    xf = x.astype(jnp.float32)
    y = xf * lax.rsqrt(jnp.mean(xf * xf, axis=-1, keepdims=True) + EPS)
    return (y * w.astype(jnp.float32)).astype(x.dtype)


def modulate(u, shift, scale):
    return u * (1.0 + scale) + shift


def swiglu(u, w_gate, w_up, w_down):
    return (jax.nn.silu(u @ w_gate) * (u @ w_up)) @ w_down


def centred_dwconv(u, w, b):
    pad = SSD_CONV // 2
    length = u.shape[-2]
    up = jnp.pad(u, [(0, 0)] * (u.ndim - 2) + [(pad, pad), (0, 0)])
    return sum(w[k] * up[..., k:k + length, :] for k in range(SSD_CONV)) + b


def lower_tri(n):
    return jnp.tril(jnp.ones((n, n), dtype=bool))


def gla_chunk_scan(q, k, v, log_f, s0):
    bsz, length, heads = q.shape[:3]
    nc = length // HG_CHUNK
    q, k, v, log_f = (t.reshape(bsz, nc, HG_CHUNK, heads, t.shape[-1]) for t in (q, k, v, log_f))
    cum = jnp.cumsum(log_f, axis=2)
    q_dec = q * jnp.exp(cum)
    k_inv = k * jnp.exp(-cum)
    scores = jnp.where(lower_tri(HG_CHUNK), jnp.einsum('bcihf,bcjhf->bchij', q_dec, k_inv), 0.0)
    o_intra = jnp.einsum('bchij,bcjhv->bcihv', scores, v)
    k_end = k * jnp.exp(cum[:, :, -1:] - cum)
    chunk_states = jnp.einsum('bcjhf,bcjhv->bchfv', k_end, v)
    chunk_decay = jnp.exp(cum[:, :, -1]).astype(chunk_states.dtype)

    def step(s, inp):
        dec, ds = inp
        return (dec[..., None] * s + ds).astype(ds.dtype), s

    s_final, s_prev = lax.scan(step, s0.astype(chunk_states.dtype),
                               (jnp.moveaxis(chunk_decay, 1, 0), jnp.moveaxis(chunk_states, 1, 0)))
    s_prev = jnp.moveaxis(s_prev, 0, 1)
    o = o_intra + jnp.einsum('bcihf,bchfv->bcihv', q_dec, s_prev)
    return o.reshape(bsz, length, heads, -1), s_final


def ssd_chunk_scan(x, dt, bm, cm, s0, a):
    bsz, length = x.shape[:2]
    nc = length // SSD_CHUNK
    x, dt, bm, cm = (t.reshape((bsz, nc, SSD_CHUNK) + t.shape[2:]) for t in (x, dt, bm, cm))
    cum = jnp.cumsum(jnp.moveaxis(dt * a, 2, -1), axis=-1)
    seg = cum[..., :, None] - cum[..., None, :]
    decay = jnp.exp(jnp.where(lower_tri(SSD_CHUNK), seg, -jnp.inf))
    dtx = dt[..., None] * x
    scores = jnp.einsum('bcign,bcjgn->bcgij', cm, bm)
    y = jnp.einsum('bcgij,bcghij,bcjghp->bcighp', scores, decay, dtx)
    decay_to_end = jnp.exp(cum[..., -1:] - cum)
    chunk_states = jnp.einsum('bcjgn,bcghj,bcjghp->bcghpn', bm, decay_to_end, dtx)
    chunk_decay = jnp.exp(cum[..., -1]).astype(chunk_states.dtype)

    def step(s, inp):
        dec, ds = inp
        return (dec[..., None, None] * s + ds).astype(ds.dtype), s

    s_final, s_prev = lax.scan(step, s0.astype(chunk_states.dtype),
                               (jnp.moveaxis(chunk_decay, 1, 0), jnp.moveaxis(chunk_states, 1, 0)))
    s_prev = jnp.moveaxis(s_prev, 0, 1)
    y = y + jnp.einsum('bcign,bcghi,bcghpn->bcighp', cm, jnp.exp(cum), s_prev)
    return y.reshape((bsz, length) + y.shape[3:]), s_final


def prefix_scan(scan_fn, ctx_args, lat_args, s0, reverse):
    flip = (lambda t: jnp.flip(t, axis=1)) if reverse else (lambda t: t)
    y_ctx, s_ctx = scan_fn(*[flip(t) for t in ctx_args], s0)
    y_lat, _ = scan_fn(*[flip(t) for t in lat_args], s_ctx)
    return flip(y_ctx), flip(y_lat)


def hgrn2_inputs(q, f_raw, i, lb):
    bsz, length = q.shape[:2]
    q = jax.nn.silu(q).reshape(bsz, length, HG_HEADS, HG_F) * HG_F ** -0.5
    lb = lb.reshape(2, HG_HEADS, HG_F)
    f = lb + (1.0 - lb) * jax.nn.sigmoid(
        f_raw.astype(jnp.float32).reshape(bsz, length, 2, HG_HEADS, HG_F))
    v = i.reshape(bsz, length, HG_HEADS, HG_V)
    return q, 1.0 - f, jnp.log(f), v


def hgrn2_readout(o, g, norm_w):
    bsz, length = o.shape[:2]
    return rms_norm(o, norm_w).reshape(bsz, length, HG_WIDTH) * jax.nn.silu(g)


def hgrn2_mixer(ctx_cols, lat_cols, lb, norm_w, with_ctx_out):
    qc, kc, lfc, vc = hgrn2_inputs(*ctx_cols[:3], lb)
    ql, kl, lfl, vl = hgrn2_inputs(*lat_cols[:3], lb)
    s0 = jnp.zeros((qc.shape[0], HG_HEADS, HG_F, HG_V), jnp.float32)
    outs = [prefix_scan(gla_chunk_scan,
                        (qc, kc[:, :, d], vc, lfc[:, :, d]),
                        (ql, kl[:, :, d], vl, lfl[:, :, d]), s0, d == 1) for d in range(2)]
    y_lat = hgrn2_readout(outs[0][1] + outs[1][1], lat_cols[3], norm_w)
    y_ctx = hgrn2_readout(outs[0][0] + outs[1][0], ctx_cols[3], norm_w) if with_ctx_out else None
    return y_ctx, y_lat


def ssd_inputs(xbc, dt_raw, dt_bias):
    bsz, length = xbc.shape[:2]
    xs, bm, cm = split_cols(xbc, (SSD_WIDTH, SSD_GROUPS * SSD_STATE, SSD_GROUPS * SSD_STATE))
    xs = xs.reshape(bsz, length, SSD_GROUPS, SSD_HPG, SSD_HEADDIM)
    bm = bm.reshape(bsz, length, SSD_GROUPS, SSD_STATE)
    cm = cm.reshape(bsz, length, SSD_GROUPS, SSD_STATE)
    dt = jax.nn.softplus(dt_raw.astype(jnp.float32).reshape(bsz, length, 2, SSD_GROUPS, SSD_HPG)
                         + dt_bias.astype(jnp.float32).reshape(2, SSD_GROUPS, SSD_HPG))
    return xs, dt, bm, cm


def ssd_mixer(ctx_cols, lat_cols, rows, conv_w, conv_b, a_log, dt_bias, d_skip, norm_w, with_ctx_out):
    z_c, xbc_c, dt_c = ctx_cols
    z_l, xbc_l, dt_l = lat_cols
    bsz = xbc_l.shape[0]
    xbc_c = jax.nn.silu(centred_dwconv(xbc_c, conv_w, conv_b))
    xbc_l = jax.nn.silu(centred_dwconv(xbc_l.reshape(bsz, rows, GRID_W, SSD_CONV_CH), conv_w, conv_b)
                        ).reshape(bsz, rows * GRID_W, SSD_CONV_CH)
    xc, dtc, bc, cc = ssd_inputs(xbc_c, dt_c, dt_bias)
    xl, dtl, bl, cl = ssd_inputs(xbc_l, dt_l, dt_bias)
    a = -jnp.exp(a_log.astype(jnp.float32)).reshape(2, SSD_GROUPS, SSD_HPG)
    s0 = jnp.zeros((bsz, SSD_GROUPS, SSD_HPG, SSD_HEADDIM, SSD_STATE), jnp.float32)
    outs = [prefix_scan(functools.partial(ssd_chunk_scan, a=a[d]),
                        (xc, dtc[:, :, d], bc, cc),
                        (xl, dtl[:, :, d], bl, cl), s0, d == 1) for d in range(2)]
    d_h = d_skip.reshape(SSD_GROUPS, SSD_HPG, 1)

    def readout(y, xs, z):
        b, length = z.shape[:2]
        u = (y + d_h * xs).reshape(b, length, SSD_WIDTH) * jax.nn.silu(z)
        u = rms_norm(u.reshape(b, length, SSD_GROUPS, SSD_WIDTH // SSD_GROUPS),
                     norm_w.reshape(SSD_GROUPS, -1))
        return u.reshape(b, length, SSD_WIDTH)

    y_lat = readout(outs[0][1] + outs[1][1], xl, z_l)
    y_ctx = readout(outs[0][0] + outs[1][0], xc, z_c) if with_ctx_out else None
    return y_ctx, y_lat


def token_mixing(u_ctx, u_lat, rows, w_in, conv_w, conv_b, a_log, dt_bias, d_skip, ssd_norm_w,
                 lb, hgrn_norm_w, w_out, with_ctx_out):
    pc = split_cols(u_ctx @ w_in, IN_SPLITS)
    pl = split_cols(u_lat @ w_in, IN_SPLITS)
    hg_c, hg_l = hgrn2_mixer(tuple(pc[:4]), tuple(pl[:4]), lb, hgrn_norm_w, with_ctx_out)
    ss_c, ss_l = ssd_mixer(tuple(pc[4:]), tuple(pl[4:]), rows, conv_w, conv_b, a_log, dt_bias,
                           d_skip, ssd_norm_w, with_ctx_out)
    y_lat = jnp.concatenate([hg_l, ss_l], axis=-1) @ w_out
    y_ctx = jnp.concatenate([hg_c, ss_c], axis=-1) @ w_out if with_ctx_out else None
    return y_ctx, y_lat


def setup_inputs(seed: int = 0) -> dict:
    key = jax.random.key(seed)
    ks = jax.random.split(key, 24)
    nrm = lambda k, shape, s: jax.random.normal(k, shape, jnp.float32) * s
    gain = lambda k, shape: 1.0 + 0.01 * jax.random.normal(k, shape, jnp.float32)
    dt0 = jnp.exp(jax.random.uniform(ks[10], (DEPTH, 2, SSD_HEADS), jnp.float32)
                  * (math.log(0.1) - math.log(0.001)) + math.log(0.001))
    return {
        "x": nrm(ks[0], (BATCH, SEQ, D_MODEL), 1.0),
        "c": nrm(ks[1], (BATCH, D_MODEL), 1.0),
        "ctx": nrm(ks[2], (BATCH, CTX_LEN, D_MODEL), 1.0),
        "c_ctx": nrm(ks[3], (D_MODEL,), 1.0),
        "w_ada": nrm(ks[4], (DEPTH, D_MODEL, N_MOD * D_MODEL), D_MODEL ** -0.5),
        "b_ada": nrm(ks[5], (DEPTH, N_MOD * D_MODEL), 0.01),
        "norm_mix": gain(ks[6], (DEPTH, D_MODEL)),
        "w_in": nrm(ks[7], (DEPTH, D_MODEL, D_IN), D_MODEL ** -0.5),
        "conv_w": nrm(ks[8], (DEPTH, SSD_CONV, SSD_CONV_CH), SSD_CONV ** -0.5),
        "conv_b": nrm(ks[9], (DEPTH, SSD_CONV_CH), 0.01),
        "ssd_a_log": jnp.log(jax.random.uniform(ks[11], (DEPTH, 2, SSD_HEADS), jnp.float32, 1.0, 16.0)),
        "ssd_dt_bias": dt0 + jnp.log(-jnp.expm1(-dt0)),
        "ssd_d": gain(ks[12], (DEPTH, SSD_HEADS)),
        "ssd_norm": gain(ks[13], (DEPTH, SSD_WIDTH)),
        "hgrn_lb_raw": nrm(ks[14], (DEPTH + 1, 2, HG_F_WIDTH), 0.1),
        "hgrn_norm": gain(ks[15], (DEPTH, HG_V)),
        "w_out": nrm(ks[16], (DEPTH, D_MIX, D_MODEL), D_MIX ** -0.5),
        "norm_ffn": gain(ks[17], (DEPTH, D_MODEL)),
        "w_gate": nrm(ks[18], (DEPTH, D_MODEL, D_FF), D_MODEL ** -0.5),
        "w_up": nrm(ks[19], (DEPTH, D_MODEL, D_FF), D_MODEL ** -0.5),
        "w_down": nrm(ks[20], (DEPTH, D_FF, D_MODEL), D_FF ** -0.5),
        "final_norm": gain(ks[21], (D_MODEL,)),
    }


def reference(x, c, ctx, c_ctx, w_ada, b_ada, norm_mix, w_in, conv_w, conv_b, ssd_a_log,
              ssd_dt_bias, ssd_d, ssd_norm, hgrn_lb_raw, hgrn_norm, w_out, norm_ffn,
              w_gate, w_up, w_down, final_norm):
    rows = x.shape[1] // GRID_W
    lbs = jnp.cumsum(jax.nn.softmax(hgrn_lb_raw.astype(jnp.float32), axis=0), axis=0)
    c_act = jax.nn.silu(c)
    cc_act = jax.nn.silu(c_ctx)
    h, hc = x, ctx
    for layer in range(DEPTH):
        with_ctx_out = layer < DEPTH - 1
        mod_lat = (c_act @ w_ada[layer] + b_ada[layer])[:, None, :]
        mod_ctx = cc_act @ w_ada[layer] + b_ada[layer]
        sh1, sc1, g1, sh2, sc2, g2 = jnp.split(mod_lat, N_MOD, axis=-1)
        csh1, csc1, cg1, csh2, csc2, cg2 = jnp.split(mod_ctx, N_MOD, axis=-1)
        u_lat = modulate(rms_norm(h, norm_mix[layer]), sh1, sc1)
        u_ctx = modulate(rms_norm(hc, norm_mix[layer]), csh1, csc1)
        y_ctx, y_lat = token_mixing(u_ctx, u_lat, rows, w_in[layer], conv_w[layer], conv_b[layer],
                                    ssd_a_log[layer], ssd_dt_bias[layer], ssd_d[layer],
                                    ssd_norm[layer], lbs[layer], hgrn_norm[layer], w_out[layer],
                                    with_ctx_out)
        h = (h + g1 * y_lat).astype(x.dtype)
        u = modulate(rms_norm(h, norm_ffn[layer]), sh2, sc2)
        h = (h + g2 * swiglu(u, w_gate[layer], w_up[layer], w_down[layer])).astype(x.dtype)
        if with_ctx_out:
            hc = (hc + cg1 * y_ctx).astype(ctx.dtype)
            uc = modulate(rms_norm(hc, norm_ffn[layer]), csh2, csc2)
            hc = (hc + cg2 * swiglu(uc, w_gate[layer], w_up[layer], w_down[layer])).astype(ctx.dtype)
    return rms_norm(h, final_norm)
```

```python
import functools

import jax
import jax.numpy as jnp
from jax import lax
from jax.experimental import pallas as pl
from jax.experimental.pallas import tpu as pltpu

F32 = jnp.float32
BF16 = jnp.bfloat16

D_MODEL = 1024
BATCH = 8
SEQ = 4096
CTX_LEN = 256
GRID_W = 64
HG_HEADS = 8
HG_F = 128
HG_CHUNK = 64
SSD_HEADS = 16
SSD_HEADDIM = 64
SSD_GROUPS = 4
SSD_HPG = SSD_HEADS // SSD_GROUPS
SSD_STATE = 128
SSD_CONV = 5
SSD_CHUNK = 128
SSD_WIDTH = SSD_HEADS * SSD_HEADDIM
SSD_BC = SSD_GROUPS * SSD_STATE
D_FF = 2816
N_MOD = 6
EPS = 1e-6

LANES = 128
SUBLANES = 8

TILE = 256
N_LAT_TILES = SEQ // TILE
N_TILES = N_LAT_TILES + 1
MOD_ROWS = 16
CTX_ROW = BATCH

COL_Q = 0
COL_F = 1024
COL_I = 3072
COL_G = 4096
COL_Z = 5120
COL_XBC = 6144
COL_DT = 8192
D_IN = 8224
D_IN_PAD = COL_DT + LANES
PROJ_CHUNK = 512
CONV_HALO = SUBLANES
FFN_CHUNK = 256

VMEM_LIMIT = 56 * 1024 * 1024


def _silu(x):
    return x * jax.nn.sigmoid(x)


def _rms(x, w):
    return x * lax.rsqrt(jnp.mean(x * x, axis=-1, keepdims=True) + EPS) * w


def _dot(a, b):
    return jnp.dot(a, b, preferred_element_type=F32)


def _dot_nt(a, b):
    return lax.dot_general(a, b, (((1,), (1,)), ((), ())), preferred_element_type=F32)


def _dot_tn(a, b):
    return lax.dot_general(a, b, (((0,), (0,)), ((), ())), preferred_element_type=F32)


def _dot_exact(a, b):
    return jnp.dot(a, b, preferred_element_type=F32, precision=lax.Precision.HIGHEST)


def _const_spec(shape):
    nd = len(shape)
    return pl.BlockSpec(shape, lambda *_: (0,) * nd, pipeline_mode=pl.Buffered(1))


def _ada_kernel(c_ref, w_ref, b_ref, o_ref):
    act = _silu(c_ref[...]).astype(BF16)
    o_ref[...] = _dot(act, w_ref[...].astype(BF16)) + b_ref[...]


def _ada(c_all, w_ada, b_ada):
    n = N_MOD * D_MODEL
    bn = 1536
    return pl.pallas_call(
        _ada_kernel,
        grid=(n // bn,),
        in_specs=[
            pl.BlockSpec((MOD_ROWS, D_MODEL), lambda j: (0, 0)),
            pl.BlockSpec((D_MODEL, bn), lambda j: (0, j)),
            pl.BlockSpec((1, bn), lambda j: (0, j)),
        ],
        out_specs=pl.BlockSpec((MOD_ROWS, bn), lambda j: (0, j)),
        out_shape=jax.ShapeDtypeStruct((MOD_ROWS, n), F32),
        compiler_params=pltpu.CompilerParams(vmem_limit_bytes=VMEM_LIMIT),
        name="ada",
    )(c_all, w_ada, b_ada)


def _in_proj_kernel(x_ref, ctx_ref, mod_ref, nw_ref, w_ref, cw_ref, cb_ref, dtb_ref,
                    q_ref, f_ref, v_ref, g_ref, z_ref, xs_ref, bc_ref, dt_ref,
                    u_s, xbc_s):
    t = pl.program_id(1)
    is_ctx = t == 0
    x = jnp.where(is_ctx, ctx_ref[...], x_ref[...])
    shift = mod_ref[:, 0:D_MODEL]
    scale = mod_ref[:, D_MODEL:2 * D_MODEL]
    u_s[...] = (_rms(x, nw_ref[...]) * (1.0 + scale) + shift).astype(BF16)

    def proj(c0, width):
        return _dot(u_s[...], w_ref[:, c0:c0 + width])

    for c0 in range(COL_Q, COL_F, PROJ_CHUNK):
        q_ref[:, c0 - COL_Q:c0 - COL_Q + PROJ_CHUNK] = _silu(proj(c0, PROJ_CHUNK)) * (HG_F ** -0.5)
    for c0 in range(COL_F, COL_I, PROJ_CHUNK):
        f_ref[:, c0 - COL_F:c0 - COL_F + PROJ_CHUNK] = proj(c0, PROJ_CHUNK)
    for c0 in range(COL_I, COL_G, PROJ_CHUNK):
        v_ref[:, c0 - COL_I:c0 - COL_I + PROJ_CHUNK] = proj(c0, PROJ_CHUNK).astype(BF16)
    for c0 in range(COL_G, COL_Z, PROJ_CHUNK):
        g_ref[:, c0 - COL_G:c0 - COL_G + PROJ_CHUNK] = _silu(proj(c0, PROJ_CHUNK))
    for c0 in range(COL_Z, COL_XBC, PROJ_CHUNK):
        z_ref[:, c0 - COL_Z:c0 - COL_Z + PROJ_CHUNK] = _silu(proj(c0, PROJ_CHUNK))
    dt_ref[...] = jax.nn.softplus(proj(COL_DT, LANES) + dtb_ref[...])

    zero_halo = jnp.zeros((CONV_HALO, SSD_WIDTH + 2 * SSD_BC), F32)
    xbc_s[0:CONV_HALO, :] = zero_halo
    xbc_s[CONV_HALO + TILE:, :] = zero_halo
    for c0 in range(COL_XBC, COL_DT, PROJ_CHUNK):
        xbc_s[CONV_HALO:CONV_HALO + TILE, c0 - COL_XBC:c0 - COL_XBC + PROJ_CHUNK] = proj(c0, PROJ_CHUNK)

    row_len = jnp.where(is_ctx, CTX_LEN, GRID_W)
    pos = lax.broadcasted_iota(jnp.int32, (TILE, LANES), 0) & (row_len - 1)
    pad = SSD_CONV // 2
    valid = [(pos + (k - pad) >= 0) & (pos + (k - pad) < row_len) for k in range(SSD_CONV)]
    for j in range((SSD_WIDTH + 2 * SSD_BC) // LANES):
        cols = slice(j * LANES, (j + 1) * LANES)
        acc = jnp.broadcast_to(cb_ref[:, cols], (TILE, LANES))
        for k in range(SSD_CONV):
            win = xbc_s[CONV_HALO + k - pad:CONV_HALO + k - pad + TILE, cols]
            acc = acc + cw_ref[k:k + 1, cols] * jnp.where(valid[k], win, 0.0)
        act = _silu(acc)
        if j < SSD_WIDTH // LANES:
            xs_ref[:, cols] = act
        else:
            bc_ref[:, j * LANES - SSD_WIDTH:(j + 1) * LANES - SSD_WIDTH] = act.astype(BF16)


def _in_proj(x, ctx, mod3, norm_w, w_in, conv_w, conv_b, dt_bias):
    lat = lambda b, t: (b, jnp.maximum(t - 1, 0), 0)
    tok = lambda b, t: (b, t, 0)
    n_tok = N_TILES * TILE

    def out(width, dtype):
        return jax.ShapeDtypeStruct((BATCH, n_tok, width), dtype), pl.BlockSpec((None, TILE, width), tok)

    outs = [out(D_MODEL, F32), out(2 * D_MODEL, F32), out(D_MODEL, BF16), out(D_MODEL, F32),
            out(D_MODEL, F32), out(SSD_WIDTH, F32), out(2 * SSD_BC, BF16), out(LANES, F32)]
    return pl.pallas_call(
        _in_proj_kernel,
        grid=(BATCH, N_TILES),
        in_specs=[
            pl.BlockSpec((None, TILE, D_MODEL), lat),
            pl.BlockSpec((None, CTX_LEN, D_MODEL), lambda b, t: (b, 0, 0)),
            pl.BlockSpec((None, 1, N_MOD * D_MODEL), lambda b, t: (jnp.where(t == 0, CTX_ROW, b), 0, 0)),
            _const_spec((1, D_MODEL)),
            _const_spec((D_MODEL, D_IN_PAD)),
            _const_spec((SSD_CONV, SSD_WIDTH + 2 * SSD_BC)),
            _const_spec((1, SSD_WIDTH + 2 * SSD_BC)),
            _const_spec((1, LANES)),
        ],
        out_specs=[o[1] for o in outs],
        out_shape=[o[0] for o in outs],
        scratch_shapes=[
            pltpu.VMEM((TILE, D_MODEL), BF16),
            pltpu.VMEM((TILE + 2 * CONV_HALO, SSD_WIDTH + 2 * SSD_BC), F32),
        ],
        compiler_params=pltpu.CompilerParams(
            dimension_semantics=("parallel", "arbitrary"), vmem_limit_bytes=VMEM_LIMIT),
        name="in_proj",
    )(x, ctx, mod3, norm_w, w_in, conv_w, conv_b, dt_bias)


def _tri(n, reverse):
    i = lax.broadcasted_iota(jnp.int32, (n, n), 0)
    j = lax.broadcasted_iota(jnp.int32, (n, n), 1)
    return jnp.where((j >= i) if reverse else (j <= i), 1.0, 0.0).astype(F32)


def _hgrn_chunk(rows, q_ref, f_ref, v_ref, lb_ref, s_ref, tri, reverse):
    last = slice(0, 1) if reverse else slice(HG_CHUNK - 1, HG_CHUNK)
    outs = []
    for h in range(HG_HEADS):
        cols = slice(h * HG_F, (h + 1) * HG_F)
        lb = lb_ref[:, cols]
        f = lb + (1.0 - lb) * jax.nn.sigmoid(f_ref[rows, cols])
        k = 1.0 - f
        cum = _dot_exact(tri, jnp.log(f))
        tot = cum[last, :]
        qd = (q_ref[rows, cols] * jnp.exp(cum)).astype(BF16)
        ki = (k * jnp.exp(-cum)).astype(BF16)
        ke = (k * jnp.exp(tot - cum)).astype(BF16)
        v = v_ref[rows, cols]
        sc = jnp.where(tri > 0.0, _dot_nt(qd, ki), 0.0).astype(BF16)
        s_t = s_ref[h]
        outs.append(_dot(sc, v) + _dot_nt(qd, s_t.astype(BF16)))
        s_ref[h] = s_t * jnp.exp(tot) + _dot_tn(v, ke)
    return outs


def _lane_pair(col_a, col_b):
    n = col_a.shape[0]
    lane = lax.broadcasted_iota(jnp.int32, (n, LANES), 1)
    return jnp.where(lane < SSD_HEADDIM, col_a, col_b)


def _ssd_chunk(rows, xs_ref, bc_ref, dt_ref, a_ref, s_ref, tri, tri_t, reverse, dcol):
    last = slice(0, 1) if reverse else slice(SSD_CHUNK - 1, SSD_CHUNK)
    dt = dt_ref[rows, :]
    la = dt * a_ref[...]
    cum = _dot_exact(tri, la)
    cum_t = lax.dot_general(la, tri_t, (((0,), (0,)), ((), ())),
                            preferred_element_type=F32, precision=lax.Precision.HIGHEST)
    tot = cum[last, :]
    e_cum = jnp.exp(cum)
    e_end = jnp.exp(tot - cum)
    e_tot = jnp.exp(tot)
    mask = tri > 0.0
    slabs = []
    for g in range(SSD_GROUPS):
        bg = bc_ref[rows, g * SSD_STATE:(g + 1) * SSD_STATE]
        cg = bc_ref[rows, SSD_BC + g * SSD_STATE:SSD_BC + (g + 1) * SSD_STATE]
        sc = _dot_nt(cg, bg)
        grp = slice(g * SSD_HPG * SSD_HEADDIM, (g + 1) * SSD_HPG * SSD_HEADDIM)
        y_inter = _dot_nt(cg, s_ref[grp, :].astype(BF16))
        dtx_end = []
        for pair in range(SSD_HPG // 2):
            h0 = g * SSD_HPG + 2 * pair
            c0, c1 = dcol + h0, dcol + h0 + 1
            cols = slice(h0 * SSD_HEADDIM, (h0 + 2) * SSD_HEADDIM)
            dtx = _lane_pair(dt[:, c0:c0 + 1], dt[:, c1:c1 + 1]) * xs_ref[rows, cols]
            dtx_b = dtx.astype(BF16)
            ys = []
            for c in (c0, c1):
                seg = cum[:, c:c + 1] - cum_t[c:c + 1, :]
                m = (sc * jnp.exp(jnp.where(mask, seg, -jnp.inf))).astype(BF16)
                ys.append(_dot(m, dtx_b))
            lane = lax.broadcasted_iota(jnp.int32, (SSD_CHUNK, LANES), 1)
            y_intra = jnp.where(lane < SSD_HEADDIM, ys[0], ys[1])
            y_int = y_inter[:, 2 * pair * SSD_HEADDIM:(2 * pair + 2) * SSD_HEADDIM]
            slabs.append(y_intra + _lane_pair(e_cum[:, c0:c0 + 1], e_cum[:, c1:c1 + 1]) * y_int)
            dtx_end.append((_lane_pair(e_end[:, c0:c0 + 1], e_end[:, c1:c1 + 1]) * dtx).astype(BF16))
        ds = _dot_tn(jnp.concatenate(dtx_end, axis=1), bg)
        for hh in range(SSD_HPG):
            c = dcol + g * SSD_HPG + hh
            r = slice((g * SSD_HPG + hh) * SSD_HEADDIM, (g * SSD_HPG + hh + 1) * SSD_HEADDIM)
            s_ref[r, :] = s_ref[r, :] * e_tot[:, c:c + 1] + ds[hh * SSD_HEADDIM:(hh + 1) * SSD_HEADDIM, :]
    return slabs


def _fwd_scan_kernel(q_ref, f_ref, v_ref, xs_ref, bc_ref, dt_ref, lb_ref, a_ref,
                     o_ref, y_ref, shg_s, sssd_s):
    t = pl.program_id(1)

    @pl.when(t == 0)
    def _():
        shg_s[...] = jnp.zeros_like(shg_s)
        sssd_s[...] = jnp.zeros_like(sssd_s)

    tri_h = _tri(HG_CHUNK, False)
    tri_s = _tri(SSD_CHUNK, False)
    tri_st = _tri(SSD_CHUNK, True)

    def hg_body(c, carry):
        rows = pl.ds(pl.multiple_of(c * HG_CHUNK, HG_CHUNK), HG_CHUNK)
        outs = _hgrn_chunk(rows, q_ref, f_ref, v_ref, lb_ref, shg_s, tri_h, False)
        for h, o in enumerate(outs):
            o_ref[rows, h * HG_F:(h + 1) * HG_F] = o
        return carry

    lax.fori_loop(0, TILE // HG_CHUNK, hg_body, 0)

    def ssd_body(c, carry):
        rows = pl.ds(pl.multiple_of(c * SSD_CHUNK, SSD_CHUNK), SSD_CHUNK)
        slabs = _ssd_chunk(rows, xs_ref, bc_ref, dt_ref, a_ref, sssd_s, tri_s, tri_st, False, 0)
        for j, y in enumerate(slabs):
            y_ref[rows, j * LANES:(j + 1) * LANES] = y
        return carry

    lax.fori_loop(0, TILE // SSD_CHUNK, ssd_body, 0)


def _fwd_scan(q, f, v, xs, bc, dt, lb, a_pad):
    tok = lambda b, t: (b, t, 0)
    lat = lambda b, t: (b, jnp.maximum(t - 1, 0), 0)
    return pl.pallas_call(
        _fwd_scan_kernel,
        grid=(BATCH, N_TILES),
        in_specs=[
            pl.BlockSpec((None, TILE, D_MODEL), tok),
            pl.BlockSpec((None, TILE, D_MODEL), tok),
            pl.BlockSpec((None, TILE, D_MODEL), tok),
            pl.BlockSpec((None, TILE, SSD_WIDTH), tok),
            pl.BlockSpec((None, TILE, 2 * SSD_BC), tok),
            pl.BlockSpec((None, TILE, LANES), tok),
            pl.BlockSpec((None, 1, D_MODEL), lambda b, t: (0, 0, 0)),
            _const_spec((1, LANES)),
        ],
        out_specs=[pl.BlockSpec((None, TILE, D_MODEL), lat), pl.BlockSpec((None, TILE, SSD_WIDTH), lat)],
        out_shape=[jax.ShapeDtypeStruct((BATCH, SEQ, D_MODEL), F32),
                   jax.ShapeDtypeStruct((BATCH, SEQ, SSD_WIDTH), F32)],
        scratch_shapes=[
            pltpu.VMEM((HG_HEADS, HG_F, HG_F), F32),
            pltpu.VMEM((SSD_HEADS * SSD_HEADDIM, SSD_STATE), F32),
        ],
        compiler_params=pltpu.CompilerParams(
            dimension_semantics=("parallel", "arbitrary"), vmem_limit_bytes=VMEM_LIMIT),
        name="fwd_scan",
    )(q, f, v, xs, bc, dt, lb, a_pad)


def _bwd_scan_kernel(q_ref, f_ref, v_ref, xs_ref, bc_ref, dt_ref, g_ref, z_ref, of_ref, yf_ref,
                     lb_ref, a_ref, hnw_ref, dsk_ref, snw_ref, out_ref, shg_s, sssd_s):
    s = pl.program_id(1)

    @pl.when(s == 0)
    def _():
        shg_s[...] = jnp.zeros_like(shg_s)
        sssd_s[...] = jnp.zeros_like(sssd_s)

    tri_h = _tri(HG_CHUNK, True)
    tri_s = _tri(SSD_CHUNK, True)
    tri_st = _tri(SSD_CHUNK, False)

    def hg_body(i, carry):
        c = TILE // HG_CHUNK - 1 - i
        rows = pl.ds(pl.multiple_of(c * HG_CHUNK, HG_CHUNK), HG_CHUNK)
        outs = _hgrn_chunk(rows, q_ref, f_ref, v_ref, lb_ref, shg_s, tri_h, True)
        for h, o_b in enumerate(outs):
            cols = slice(h * HG_F, (h + 1) * HG_F)
            o = of_ref[rows, cols] + o_b
            out_ref[rows, cols] = (_rms(o, hnw_ref[...]) * g_ref[rows, cols]).astype(BF16)
        return carry

    lax.fori_loop(0, TILE // HG_CHUNK, hg_body, 0)

    def ssd_body(i, carry):
        c = TILE // SSD_CHUNK - 1 - i
        rows = pl.ds(pl.multiple_of(c * SSD_CHUNK, SSD_CHUNK), SSD_CHUNK)
        slabs = _ssd_chunk(rows, xs_ref, bc_ref, dt_ref, a_ref, sssd_s, tri_s, tri_st, True, SSD_HEADS)
        per_group = SSD_HPG * SSD_HEADDIM // LANES
        for g in range(SSD_GROUPS):
            us = []
            for j in range(g * per_group, (g + 1) * per_group):
                cols = slice(j * LANES, (j + 1) * LANES)
                y = yf_ref[rows, cols] + slabs[j] + dsk_ref[:, cols] * xs_ref[rows, cols]
                us.append(y * z_ref[rows, cols])
            ms = sum(jnp.sum(u * u, axis=-1, keepdims=True) for u in us) / (per_group * LANES)
            inv = lax.rsqrt(ms + EPS)
            for j, u in zip(range(g * per_group, (g + 1) * per_group), us):
                cols = slice(j * LANES, (j + 1) * LANES)
                out_ref[rows, SSD_WIDTH + j * LANES:SSD_WIDTH + (j + 1) * LANES] = (
                    u * inv * snw_ref[:, cols]).astype(BF16)
        return carry

    lax.fori_loop(0, TILE // SSD_CHUNK, ssd_body, 0)


def _bwd_scan(q, f, v, xs, bc, dt, g, z, o_f, y_f, lb, a_pad, hg_norm, d_skip, ssd_norm):
    tok = lambda b, s: (b, jnp.where(s == 0, 0, N_TILES - s), 0)
    fcol = lambda b, s: (b, jnp.where(s == 0, 0, N_TILES - s), 1)
    lat = lambda b, s: (b, jnp.where(s == 0, N_LAT_TILES - 1, N_LAT_TILES - s), 0)
    return pl.pallas_call(
        _bwd_scan_kernel,
        grid=(BATCH, N_TILES),
        in_specs=[
            pl.BlockSpec((None, TILE, D_MODEL), tok),
            pl.BlockSpec((None, TILE, D_MODEL), fcol),
            pl.BlockSpec((None, TILE, D_MODEL), tok),
            pl.BlockSpec((None, TILE, SSD_WIDTH), tok),
            pl.BlockSpec((None, TILE, 2 * SSD_BC), tok),
            pl.BlockSpec((None, TILE, LANES), tok),
            pl.BlockSpec((None, TILE, D_MODEL), tok),
            pl.BlockSpec((None, TILE, SSD_WIDTH), tok),
            pl.BlockSpec((None, TILE, D_MODEL), lat),
            pl.BlockSpec((None, TILE, SSD_WIDTH), lat),
            pl.BlockSpec((None, 1, D_MODEL), lambda b, s: (1, 0, 0)),
            _const_spec((1, LANES)),
            _const_spec((1, HG_F)),
            _const_spec((1, SSD_WIDTH)),
            _const_spec((1, SSD_WIDTH)),
        ],
        out_specs=pl.BlockSpec((None, TILE, 2 * D_MODEL), lat),
        out_shape=jax.ShapeDtypeStruct((BATCH, SEQ, 2 * D_MODEL), BF16),
        scratch_shapes=[
            pltpu.VMEM((HG_HEADS, HG_F, HG_F), F32),
            pltpu.VMEM((SSD_HEADS * SSD_HEADDIM, SSD_STATE), F32),
        ],
        compiler_params=pltpu.CompilerParams(
            dimension_semantics=("parallel", "arbitrary"), vmem_limit_bytes=VMEM_LIMIT),
        name="bwd_scan",
    )(q, f, v, xs, bc, dt, g, z, o_f, y_f, lb, a_pad, hg_norm, d_skip, ssd_norm)


def _out_ffn_kernel(y_ref, x_ref, mod_ref, wo_ref, nf_ref, wg_ref, wu_ref, wd_ref, fn_ref,
                    out_ref, u_s, acc_s):
    g1 = mod_ref[:, 2 * D_MODEL:3 * D_MODEL]
    sh2 = mod_ref[:, 3 * D_MODEL:4 * D_MODEL]
    sc2 = mod_ref[:, 4 * D_MODEL:5 * D_MODEL]
    g2 = mod_ref[:, 5 * D_MODEL:6 * D_MODEL]
    h = x_ref[...] + g1 * _dot(y_ref[...], wo_ref[...])
    u_s[...] = (_rms(h, nf_ref[...]) * (1.0 + sc2) + sh2).astype(BF16)
    for c0 in range(0, D_FF, FFN_CHUNK):
        gate = _dot(u_s[...], wg_ref[:, c0:c0 + FFN_CHUNK])
        up = _dot(u_s[...], wu_ref[:, c0:c0 + FFN_CHUNK])
        part = _dot((_silu(gate) * up).astype(BF16), wd_ref[c0:c0 + FFN_CHUNK, :])
        if c0 == 0:
            acc_s[...] = part
        else:
            acc_s[...] += part
    out_ref[...] = _rms(h + g2 * acc_s[...], fn_ref[...])


def _out_ffn(ymix, x, mod3, w_out, norm_ffn, w_gate, w_up, w_down, final_norm):
    tok = lambda b, t: (b, t, 0)
    return pl.pallas_call(
        _out_ffn_kernel,
        grid=(BATCH, N_LAT_TILES),
        in_specs=[
            pl.BlockSpec((None, TILE, 2 * D_MODEL), tok),
            pl.BlockSpec((None, TILE, D_MODEL), tok),
            pl.BlockSpec((None, 1, N_MOD * D_MODEL), lambda b, t: (b, 0, 0)),
            _const_spec((2 * D_MODEL, D_MODEL)),
            _const_spec((1, D_MODEL)),
            _const_spec((D_MODEL, D_FF)),
            _const_spec((D_MODEL, D_FF)),
            _const_spec((D_FF, D_MODEL)),
            _const_spec((1, D_MODEL)),
        ],
        out_specs=pl.BlockSpec((None, TILE, D_MODEL), tok),
        out_shape=jax.ShapeDtypeStruct((BATCH, SEQ, D_MODEL), F32),
        scratch_shapes=[pltpu.VMEM((TILE, D_MODEL), BF16), pltpu.VMEM((TILE, D_MODEL), F32)],
        compiler_params=pltpu.CompilerParams(
            dimension_semantics=("parallel", "arbitrary"), vmem_limit_bytes=VMEM_LIMIT),
        name="out_ffn",
    )(ymix, x, mod3, w_out, norm_ffn, w_gate, w_up, w_down, final_norm)


def kernel(x, c, ctx, c_ctx, w_ada, b_ada, norm_mix, w_in, conv_w, conv_b, ssd_a_log, ssd_dt_bias,
           ssd_d, ssd_norm, hgrn_lb_raw, hgrn_norm, w_out, norm_ffn, w_gate, w_up, w_down, final_norm):
    assert x.shape == (BATCH, SEQ, D_MODEL) and ctx.shape == (BATCH, CTX_LEN, D_MODEL)
    assert w_ada.shape == (1, D_MODEL, N_MOD * D_MODEL) and w_in.shape == (1, D_MODEL, D_IN)

    c_all = jnp.concatenate([c, c_ctx[None, :], jnp.zeros((MOD_ROWS - BATCH - 1, D_MODEL), F32)], axis=0)
    w_in_b = jnp.pad(w_in[0].astype(BF16), ((0, 0), (0, D_IN_PAD - D_IN)))
    dt_bias = jnp.pad(ssd_dt_bias[0].reshape(1, 2 * SSD_HEADS), ((0, 0), (0, LANES - 2 * SSD_HEADS)))
    a_pad = jnp.pad(-jnp.exp(ssd_a_log[0].astype(F32)).reshape(1, 2 * SSD_HEADS),
                    ((0, 0), (0, LANES - 2 * SSD_HEADS)))
    lbs = jnp.cumsum(jax.nn.softmax(hgrn_lb_raw.astype(F32), axis=0), axis=0)[0][:, None, :]
    d_skip = jnp.repeat(ssd_d[0], SSD_HEADDIM)[None, :]

    mod = _ada(c_all, w_ada[0], b_ada)
    mod3 = mod.reshape(MOD_ROWS, 1, N_MOD * D_MODEL)
    q, f, v, g, z, xs, bc, dt = _in_proj(x, ctx, mod3, norm_mix, w_in_b, conv_w[0], conv_b, dt_bias)
    o_f, y_f = _fwd_scan(q, f, v, xs, bc, dt, lbs, a_pad)
    ymix = _bwd_scan(q, f, v, xs, bc, dt, g, z, o_f, y_f, lbs, a_pad,
                     hgrn_norm, d_skip, ssd_norm)
    return _out_ffn(ymix, x, mod3, w_out[0].astype(BF16), norm_ffn, w_gate[0].astype(BF16),
                    w_up[0].astype(BF16), w_down[0].astype(BF16), final_norm[None, :])
```

```python
import jax
import jax.numpy as jnp
from jax import lax
from jax.experimental import pallas as pl
from jax.experimental.pallas import tpu as pltpu

F32 = jnp.float32
BF16 = jnp.bfloat16

D_MODEL = 1024
BATCH = 8
SEQ = 4096
CTX_LEN = 256
GRID_W = 64
HG_HEADS = 8
HG_F = 128
HG_CHUNK = 64
SSD_HEADS = 16
SSD_HEADDIM = 64
SSD_GROUPS = 4
SSD_HPG = SSD_HEADS // SSD_GROUPS
SSD_STATE = 128
SSD_CONV = 5
SSD_CHUNK = 128
SSD_WIDTH = SSD_HEADS * SSD_HEADDIM
SSD_BC = SSD_GROUPS * SSD_STATE
D_FF = 2816
N_MOD = 6
EPS = 1e-6

LANES = 128
SUBLANES = 8

TILE = 256
N_LAT_TILES = SEQ // TILE
N_TILES = N_LAT_TILES + 1
MOD_ROWS = 16
CTX_ROW = BATCH
HG_NC = TILE // HG_CHUNK
SSD_NC = TILE // SSD_CHUNK
HG_PAIR = 2 * HG_F
SSD_GRP_W = SSD_HPG * SSD_HEADDIM

COL_Q = 0
COL_F = 1024
COL_I = 3072
COL_G = 4096
COL_Z = 5120
COL_XBC = 6144
COL_DT = 8192
D_IN = 8224
D_IN_PAD = COL_DT + LANES
PROJ_CHUNK = 512
CONV_HALO = SUBLANES
FFN_CHUNK = 256

VMEM_LIMIT = 56 * 1024 * 1024


def _silu(x):
    return x * jax.nn.sigmoid(x)


def _rms(x, w):
    return x * lax.rsqrt(jnp.mean(x * x, axis=-1, keepdims=True) + EPS) * w


def _dot(a, b):
    return jnp.dot(a, b, preferred_element_type=F32)


def _dot_nt(a, b):
    return lax.dot_general(a, b, (((1,), (1,)), ((), ())), preferred_element_type=F32)


def _dot_tn(a, b):
    return lax.dot_general(a, b, (((0,), (0,)), ((), ())), preferred_element_type=F32)


def _cumsum_mm(tri_b, x):
    hi = x.astype(BF16)
    r1 = x - hi.astype(F32)
    mid = r1.astype(BF16)
    lo = (r1 - mid.astype(F32)).astype(BF16)
    return _dot(tri_b, hi) + _dot(tri_b, mid) + _dot(tri_b, lo)


def _block_diag(a, b):
    z = jnp.zeros_like(a)
    return jnp.concatenate([jnp.concatenate([a, z], axis=1), jnp.concatenate([z, b], axis=1)], axis=0)


def _lane_pair(col_a, col_b):
    n = col_a.shape[0]
    lane = lax.broadcasted_iota(jnp.int32, (n, LANES), 1)
    return jnp.where(lane < SSD_HEADDIM, col_a, col_b)


def _const_spec(shape):
    nd = len(shape)
    return pl.BlockSpec(shape, lambda *_: (0,) * nd, pipeline_mode=pl.Buffered(1))


def _ada_kernel(c_ref, w_ref, b_ref, o_ref):
    act = _silu(c_ref[...]).astype(BF16)
    o_ref[...] = _dot(act, w_ref[...].astype(BF16)) + b_ref[...]


def _ada(c_all, w_ada, b_ada):
    n = N_MOD * D_MODEL
    bn = 1536
    return pl.pallas_call(
        _ada_kernel,
        grid=(n // bn,),
        in_specs=[
            pl.BlockSpec((MOD_ROWS, D_MODEL), lambda j: (0, 0)),
            pl.BlockSpec((D_MODEL, bn), lambda j: (0, j)),
            pl.BlockSpec((1, bn), lambda j: (0, j)),
        ],
        out_specs=pl.BlockSpec((MOD_ROWS, bn), lambda j: (0, j)),
        out_shape=jax.ShapeDtypeStruct((MOD_ROWS, n), F32),
        compiler_params=pltpu.CompilerParams(vmem_limit_bytes=VMEM_LIMIT),
        name="ada",
    )(c_all, w_ada, b_ada)


def _in_proj_kernel(x_ref, ctx_ref, mod_ref, nw_ref, w_ref, cw_ref, cb_ref, dtb_ref,
                    q_ref, f_ref, v_ref, g_ref, z_ref, xs_ref, dxf_ref, dxb_ref, bc_ref, dt_ref,
                    u_s, xbc_s):
    t = pl.program_id(1)
    is_ctx = t == 0
    x = jnp.where(is_ctx, ctx_ref[...], x_ref[...])
    shift = mod_ref[:, 0:D_MODEL]
    scale = mod_ref[:, D_MODEL:2 * D_MODEL]
    u_s[...] = (_rms(x, nw_ref[...]) * (1.0 + scale) + shift).astype(BF16)

    def proj(c0, width):
        return _dot(u_s[...], w_ref[:, c0:c0 + width])

    for c0 in range(COL_Q, COL_F, PROJ_CHUNK):
        q_ref[:, c0 - COL_Q:c0 - COL_Q + PROJ_CHUNK] = _silu(proj(c0, PROJ_CHUNK)) * (HG_F ** -0.5)
    for c0 in range(COL_F, COL_I, PROJ_CHUNK):
        f_ref[:, c0 - COL_F:c0 - COL_F + PROJ_CHUNK] = proj(c0, PROJ_CHUNK)
    for c0 in range(COL_I, COL_G, PROJ_CHUNK):
        v_ref[:, c0 - COL_I:c0 - COL_I + PROJ_CHUNK] = proj(c0, PROJ_CHUNK).astype(BF16)
    for c0 in range(COL_G, COL_Z, PROJ_CHUNK):
        g_ref[:, c0 - COL_G:c0 - COL_G + PROJ_CHUNK] = _silu(proj(c0, PROJ_CHUNK))
    for c0 in range(COL_Z, COL_XBC, PROJ_CHUNK):
        z_ref[:, c0 - COL_Z:c0 - COL_Z + PROJ_CHUNK] = _silu(proj(c0, PROJ_CHUNK))
    dt = jax.nn.softplus(proj(COL_DT, LANES) + dtb_ref[...])
    dt_ref[...] = dt

    zero_halo = jnp.zeros((CONV_HALO, SSD_WIDTH + 2 * SSD_BC), F32)
    xbc_s[0:CONV_HALO, :] = zero_halo
    xbc_s[CONV_HALO + TILE:, :] = zero_halo
    for c0 in range(COL_XBC, COL_DT, PROJ_CHUNK):
        xbc_s[CONV_HALO:CONV_HALO + TILE, c0 - COL_XBC:c0 - COL_XBC + PROJ_CHUNK] = proj(c0, PROJ_CHUNK)

    row_len = jnp.where(is_ctx, CTX_LEN, GRID_W)
    pos = lax.broadcasted_iota(jnp.int32, (TILE, LANES), 0) & (row_len - 1)
    pad = SSD_CONV // 2
    valid = [(pos + (k - pad) >= 0) & (pos + (k - pad) < row_len) for k in range(SSD_CONV)]
    for j in range((SSD_WIDTH + 2 * SSD_BC) // LANES):
        cols = slice(j * LANES, (j + 1) * LANES)
        acc = jnp.broadcast_to(cb_ref[:, cols], (TILE, LANES))
        for k in range(SSD_CONV):
            win = xbc_s[CONV_HALO + k - pad:CONV_HALO + k - pad + TILE, cols]
            acc = acc + cw_ref[k:k + 1, cols] * jnp.where(valid[k], win, 0.0)
        act = _silu(acc)
        if j < SSD_WIDTH // LANES:
            xs_ref[:, cols] = act
            h0 = 2 * j
            dxf_ref[:, cols] = (_lane_pair(dt[:, h0:h0 + 1], dt[:, h0 + 1:h0 + 2]) * act).astype(BF16)
            hb = SSD_HEADS + h0
            dxb_ref[:, cols] = (_lane_pair(dt[:, hb:hb + 1], dt[:, hb + 1:hb + 2]) * act).astype(BF16)
        else:
            bc_ref[:, j * LANES - SSD_WIDTH:(j + 1) * LANES - SSD_WIDTH] = act.astype(BF16)


def _in_proj(x, ctx, mod3, norm_w, w_in, conv_w, conv_b, dt_bias):
    lat = lambda b, t: (b, jnp.maximum(t - 1, 0), 0)
    tok = lambda b, t: (b, t, 0)
    n_tok = N_TILES * TILE

    def out(width, dtype):
        return jax.ShapeDtypeStruct((BATCH, n_tok, width), dtype), pl.BlockSpec((None, TILE, width), tok)

    outs = [out(D_MODEL, F32), out(2 * D_MODEL, F32), out(D_MODEL, BF16), out(D_MODEL, F32),
            out(D_MODEL, F32), out(SSD_WIDTH, F32), out(SSD_WIDTH, BF16), out(SSD_WIDTH, BF16),
            out(2 * SSD_BC, BF16), out(LANES, F32)]
    return pl.pallas_call(
        _in_proj_kernel,
        grid=(BATCH, N_TILES),
        in_specs=[
            pl.BlockSpec((None, TILE, D_MODEL), lat),
            pl.BlockSpec((None, CTX_LEN, D_MODEL), lambda b, t: (b, 0, 0)),
            pl.BlockSpec((None, 1, N_MOD * D_MODEL), lambda b, t: (jnp.where(t == 0, CTX_ROW, b), 0, 0)),
            _const_spec((1, D_MODEL)),
            _const_spec((D_MODEL, D_IN_PAD)),
            _const_spec((SSD_CONV, SSD_WIDTH + 2 * SSD_BC)),
            _const_spec((1, SSD_WIDTH + 2 * SSD_BC)),
            _const_spec((1, LANES)),
        ],
        out_specs=[o[1] for o in outs],
        out_shape=[o[0] for o in outs],
        scratch_shapes=[
            pltpu.VMEM((TILE, D_MODEL), BF16),
            pltpu.VMEM((TILE + 2 * CONV_HALO, SSD_WIDTH + 2 * SSD_BC), F32),
        ],
        compiler_params=pltpu.CompilerParams(
            dimension_semantics=("parallel", "arbitrary"), vmem_limit_bytes=VMEM_LIMIT),
        name="in_proj",
    )(x, ctx, mod3, norm_w, w_in, conv_w, conv_b, dt_bias)


def _tri(n, reverse):
    i = lax.broadcasted_iota(jnp.int32, (n, n), 0)
    j = lax.broadcasted_iota(jnp.int32, (n, n), 1)
    return (j >= i) if reverse else (j <= i)


def _hgrn_tile(q_ref, f_ref, v_ref, lb_ref, s_ref, qd_s, ki_s, ke_s, sc_s, sp_s, et_s, reverse, emit):
    mask = _tri(HG_CHUNK, reverse)
    tri_b = jnp.where(mask, 1.0, 0.0).astype(BF16)
    i2 = lax.broadcasted_iota(jnp.int32, (HG_CHUNK, LANES), 0)
    j2 = lax.broadcasted_iota(jnp.int32, (HG_CHUNK, LANES), 1) & (HG_CHUNK - 1)
    mask2 = (j2 >= i2) if reverse else (j2 <= i2)
    last = slice(0, 1) if reverse else slice(HG_CHUNK - 1, HG_CHUNK)
    chunk_rows = [slice(c * HG_CHUNK, (c + 1) * HG_CHUNK) for c in range(HG_NC)]
    pair_cols = [slice(p * HG_PAIR, (p + 1) * HG_PAIR) for p in range(HG_HEADS // 2)]
    order = list(reversed(range(HG_NC))) if reverse else list(range(HG_NC))

    for c in range(HG_NC):
        rows = chunk_rows[c]
        for cols in pair_cols:
            lb = lb_ref[:, cols]
            f = lb + (1.0 - lb) * jax.nn.sigmoid(f_ref[rows, cols])
            k = 1.0 - f
            cum = _cumsum_mm(tri_b, jnp.log(f))
            tot = cum[last, :]
            qd_s[rows, cols] = (q_ref[rows, cols] * jnp.exp(cum)).astype(BF16)
            ki_s[rows, cols] = (k * jnp.exp(-cum)).astype(BF16)
            ke_s[rows, cols] = (k * jnp.exp(tot - cum)).astype(BF16)
            et_s[c * SUBLANES:c * SUBLANES + 1, cols] = jnp.exp(tot)

    for c in range(HG_NC):
        rows = chunk_rows[c]
        for p, cols in enumerate(pair_cols):
            kp = ki_s[rows, cols]
            sc = _dot_nt(qd_s[rows, cols], _block_diag(kp[:, :HG_F], kp[:, HG_F:]))
            sc_s[rows, p * LANES:(p + 1) * LANES] = jnp.where(mask2, sc, 0.0).astype(BF16)

    for h in range(HG_HEADS):
        cols = slice(h * HG_F, (h + 1) * HG_F)
        s = s_ref[h]
        for c in order:
            rows = chunk_rows[c]
            sp_s[c, h] = s.astype(BF16)
            s = s * et_s[c * SUBLANES:c * SUBLANES + 1, cols] + _dot_tn(v_ref[rows, cols], ke_s[rows, cols])
        s_ref[h] = s

    for c in range(HG_NC):
        rows = chunk_rows[c]
        for p, cols in enumerate(pair_cols):
            vp = v_ref[rows, cols]
            o = _dot(sc_s[rows, p * LANES:(p + 1) * LANES], _block_diag(vp[:, :HG_F], vp[:, HG_F:]))
            o = o + _dot_nt(qd_s[rows, cols], _block_diag(sp_s[c, 2 * p], sp_s[c, 2 * p + 1]))
            emit(rows, p, o)


def _ssd_tile(dtx_ref, bc_ref, dt_ref, a_ref, s_ref, cum_s, cumt_s, sc_s, m_s, dxe_s, ecum_s, sp_s,
              reverse, dcol, emit):
    mask = _tri(SSD_CHUNK, reverse)
    tri_b = jnp.where(mask, 1.0, 0.0).astype(BF16)
    last = slice(0, 1) if reverse else slice(SSD_CHUNK - 1, SSD_CHUNK)
    chunk_rows = [slice(c * SSD_CHUNK, (c + 1) * SSD_CHUNK) for c in range(SSD_NC)]
    order = list(reversed(range(SSD_NC))) if reverse else list(range(SSD_NC))
    lane = lax.broadcasted_iota(jnp.int32, (SSD_CHUNK, LANES), 1)
    low = lane < SSD_HEADDIM

    for c in range(SSD_NC):
        cum = _cumsum_mm(tri_b, dt_ref[chunk_rows[c], :] * a_ref[...])
        cum_s[c] = cum
        cumt_s[c] = cum.T

    for c in range(SSD_NC):
        rows = chunk_rows[c]
        for g in range(SSD_GROUPS):
            bg = bc_ref[rows, g * SSD_STATE:(g + 1) * SSD_STATE]
            cg = bc_ref[rows, SSD_BC + g * SSD_STATE:SSD_BC + (g + 1) * SSD_STATE]
            sc_s[c, g] = _dot_nt(cg, bg)

    for c in range(SSD_NC):
        rows = chunk_rows[c]
        cum = cum_s[c]
        cum_t = cumt_s[c]
        for pair in range(SSD_HEADS // 2):
            e_cum, e_end = [], []
            for h in (2 * pair, 2 * pair + 1):
                col = dcol + h
                r = jnp.broadcast_to(cum[:, col:col + 1], (SSD_CHUNK, SSD_CHUNK))
                row_t = cum_t[col:col + 1, :]
                decay = jnp.exp(jnp.where(mask, r - row_t, -jnp.inf))
                m_s[c, h] = (sc_s[c, h // SSD_HPG] * decay).astype(BF16)
                e_cum.append(jnp.exp(r))
                e_end.append(jnp.exp(row_t[:, last.start:last.stop] - r))
            cols = slice(pair * LANES, (pair + 1) * LANES)
            ecum_s[rows, cols] = jnp.where(low, e_cum[0], e_cum[1])
            dxe_s[rows, cols] = (jnp.where(low, e_end[0], e_end[1]) * dtx_ref[rows, cols].astype(F32)).astype(BF16)

    for g in range(SSD_GROUPS):
        grp = slice(g * SSD_GRP_W, (g + 1) * SSD_GRP_W)
        s = s_ref[grp, :]
        for c in order:
            rows = chunk_rows[c]
            sp_s[c, grp, :] = s.astype(BF16)
            ds = _dot_tn(dxe_s[rows, grp], bc_ref[rows, g * SSD_STATE:(g + 1) * SSD_STATE])
            tot_t = cumt_s[c][:, last.start:last.stop]
            dec = jnp.concatenate(
                [jnp.broadcast_to(jnp.exp(tot_t[dcol + g * SSD_HPG + hh:dcol + g * SSD_HPG + hh + 1, :]),
                                  (SSD_HEADDIM, SSD_STATE)) for hh in range(SSD_HPG)], axis=0)
            s = s * dec + ds
        s_ref[grp, :] = s

    for c in range(SSD_NC):
        rows = chunk_rows[c]
        for g in range(SSD_GROUPS):
            grp = slice(g * SSD_GRP_W, (g + 1) * SSD_GRP_W)
            cg = bc_ref[rows, SSD_BC + g * SSD_STATE:SSD_BC + (g + 1) * SSD_STATE]
            y_inter = _dot_nt(cg, sp_s[c, grp, :])
            ys = []
            for pp in range(SSD_HPG // 2):
                pair = g * (SSD_HPG // 2) + pp
                cols = slice(pair * LANES, (pair + 1) * LANES)
                dp = dtx_ref[rows, cols]
                zero = jnp.zeros_like(dp)
                rhs = jnp.concatenate([jnp.where(low, dp, zero), jnp.where(low, zero, dp)], axis=0)
                y_intra = _dot(jnp.concatenate([m_s[c, 2 * pair], m_s[c, 2 * pair + 1]], axis=1), rhs)
                ys.append(y_intra + ecum_s[rows, cols] * y_inter[:, pp * LANES:(pp + 1) * LANES])
            emit(rows, g, ys)


_SCAN_SCRATCH = [
    pltpu.VMEM((HG_HEADS, HG_F, HG_F), F32),
    pltpu.VMEM((SSD_WIDTH, SSD_STATE), F32),
    pltpu.VMEM((TILE, D_MODEL), BF16),
    pltpu.VMEM((TILE, D_MODEL), BF16),
    pltpu.VMEM((TILE, D_MODEL), BF16),
    pltpu.VMEM((TILE, HG_HEADS // 2 * LANES), BF16),
    pltpu.VMEM((HG_NC, HG_HEADS, HG_F, HG_F), BF16),
    pltpu.VMEM((HG_NC * SUBLANES, D_MODEL), F32),
    pltpu.VMEM((SSD_NC, SSD_CHUNK, LANES), F32),
    pltpu.VMEM((SSD_NC, LANES, SSD_CHUNK), F32),
    pltpu.VMEM((SSD_NC, SSD_GROUPS, SSD_CHUNK, SSD_CHUNK), F32),
    pltpu.VMEM((SSD_NC, SSD_HEADS, SSD_CHUNK, SSD_CHUNK), BF16),
    pltpu.VMEM((TILE, SSD_WIDTH), BF16),
    pltpu.VMEM((TILE, SSD_WIDTH), F32),
    pltpu.VMEM((SSD_NC, SSD_WIDTH, SSD_STATE), BF16),
]


def _fwd_scan_kernel(q_ref, f_ref, v_ref, dtx_ref, bc_ref, dt_ref, lb_ref, a_ref,
                     o_ref, y_ref, shg_s, sssd_s, qd_s, ki_s, ke_s, hsc_s, hsp_s, et_s,
                     cum_s, cumt_s, ssc_s, m_s, dxe_s, ecum_s, ssp_s):
    t = pl.program_id(1)

    @pl.when(t == 0)
    def _():
        shg_s[...] = jnp.zeros_like(shg_s)
        sssd_s[...] = jnp.zeros_like(sssd_s)

    def emit_hg(rows, p, o):
        o_ref[rows, p * HG_PAIR:(p + 1) * HG_PAIR] = o

    def emit_ssd(rows, g, ys):
        for i, y in enumerate(ys):
            y_ref[rows, g * SSD_GRP_W + i * LANES:g * SSD_GRP_W + (i + 1) * LANES] = y

    _hgrn_tile(q_ref, f_ref, v_ref, lb_ref, shg_s, qd_s, ki_s, ke_s, hsc_s, hsp_s, et_s, False, emit_hg)
    _ssd_tile(dtx_ref, bc_ref, dt_ref, a_ref, sssd_s, cum_s, cumt_s, ssc_s, m_s, dxe_s, ecum_s, ssp_s,
              False, 0, emit_ssd)


def _fwd_scan(q, f, v, dtx, bc, dt, lb, a_pad):
    tok = lambda b, t: (b, t, 0)
    lat = lambda b, t: (b, jnp.maximum(t - 1, 0), 0)
    return pl.pallas_call(
        _fwd_scan_kernel,
        grid=(BATCH, N_TILES),
        in_specs=[
            pl.BlockSpec((None, TILE, D_MODEL), tok),
            pl.BlockSpec((None, TILE, D_MODEL), tok),
            pl.BlockSpec((None, TILE, D_MODEL), tok),
            pl.BlockSpec((None, TILE, SSD_WIDTH), tok),
            pl.BlockSpec((None, TILE, 2 * SSD_BC), tok),
            pl.BlockSpec((None, TILE, LANES), tok),
            pl.BlockSpec((None, 1, D_MODEL), lambda b, t: (0, 0, 0)),
            _const_spec((1, LANES)),
        ],
        out_specs=[pl.BlockSpec((None, TILE, D_MODEL), lat), pl.BlockSpec((None, TILE, SSD_WIDTH), lat)],
        out_shape=[jax.ShapeDtypeStruct((BATCH, SEQ, D_MODEL), F32),
                   jax.ShapeDtypeStruct((BATCH, SEQ, SSD_WIDTH), F32)],
        scratch_shapes=_SCAN_SCRATCH,
        compiler_params=pltpu.CompilerParams(
            dimension_semantics=("parallel", "arbitrary"), vmem_limit_bytes=VMEM_LIMIT),
        name="fwd_scan",
    )(q, f, v, dtx, bc, dt, lb, a_pad)


def _bwd_scan_kernel(q_ref, f_ref, v_ref, dtx_ref, xs_ref, bc_ref, dt_ref, g_ref, z_ref, of_ref, yf_ref,
                     lb_ref, a_ref, hnw_ref, dsk_ref, snw_ref, out_ref,
                     shg_s, sssd_s, qd_s, ki_s, ke_s, hsc_s, hsp_s, et_s,
                     cum_s, cumt_s, ssc_s, m_s, dxe_s, ecum_s, ssp_s):
    s = pl.program_id(1)

    @pl.when(s == 0)
    def _():
        shg_s[...] = jnp.zeros_like(shg_s)
        sssd_s[...] = jnp.zeros_like(sssd_s)

    def emit_hg(rows, p, o_b):
        for i in range(2):
            cols = slice(p * HG_PAIR + i * HG_F, p * HG_PAIR + (i + 1) * HG_F)
            o = of_ref[rows, cols] + o_b[:, i * HG_F:(i + 1) * HG_F]
            out_ref[rows, cols] = (_rms(o, hnw_ref[...]) * g_ref[rows, cols]).astype(BF16)

    def emit_ssd(rows, g, ys):
        us = []
        for i, y_b in enumerate(ys):
            cols = slice(g * SSD_GRP_W + i * LANES, g * SSD_GRP_W + (i + 1) * LANES)
            y = yf_ref[rows, cols] + y_b + dsk_ref[:, cols] * xs_ref[rows, cols]
            us.append(y * z_ref[rows, cols])
        ms = sum(jnp.sum(u * u, axis=-1, keepdims=True) for u in us) / SSD_GRP_W
        inv = lax.rsqrt(ms + EPS)
        for i, u in enumerate(us):
            cols = slice(g * SSD_GRP_W + i * LANES, g * SSD_GRP_W + (i + 1) * LANES)
            out_ref[rows, SSD_WIDTH + cols.start:SSD_WIDTH + cols.stop] = (
                u * inv * snw_ref[:, cols]).astype(BF16)

    _hgrn_tile(q_ref, f_ref, v_ref, lb_ref, shg_s, qd_s, ki_s, ke_s, hsc_s, hsp_s, et_s, True, emit_hg)
    _ssd_tile(dtx_ref, bc_ref, dt_ref, a_ref, sssd_s, cum_s, cumt_s, ssc_s, m_s, dxe_s, ecum_s, ssp_s,
              True, SSD_HEADS, emit_ssd)


def _bwd_scan(q, f, v, dtx, xs, bc, dt, g, z, o_f, y_f, lb, a_pad, hg_norm, d_skip, ssd_norm):
    tok = lambda b, s: (b, jnp.where(s == 0, 0, N_TILES - s), 0)
    fcol = lambda b, s: (b, jnp.where(s == 0, 0, N_TILES - s), 1)
    lat = lambda b, s: (b, jnp.where(s == 0, N_LAT_TILES - 1, N_LAT_TILES - s), 0)
    return pl.pallas_call(
        _bwd_scan_kernel,
        grid=(BATCH, N_TILES),
        in_specs=[
            pl.BlockSpec((None, TILE, D_MODEL), tok),
            pl.BlockSpec((None, TILE, D_MODEL), fcol),
            pl.BlockSpec((None, TILE, D_MODEL), tok),
            pl.BlockSpec((None, TILE, SSD_WIDTH), tok),
            pl.BlockSpec((None, TILE, SSD_WIDTH), tok),
            pl.BlockSpec((None, TILE, 2 * SSD_BC), tok),
            pl.BlockSpec((None, TILE, LANES), tok),
            pl.BlockSpec((None, TILE, D_MODEL), tok),
            pl.BlockSpec((None, TILE, SSD_WIDTH), tok),
            pl.BlockSpec((None, TILE, D_MODEL), lat),
            pl.BlockSpec((None, TILE, SSD_WIDTH), lat),
            pl.BlockSpec((None, 1, D_MODEL), lambda b, s: (1, 0, 0)),
            _const_spec((1, LANES)),
            _const_spec((1, HG_F)),
            _const_spec((1, SSD_WIDTH)),
            _const_spec((1, SSD_WIDTH)),
        ],
        out_specs=pl.BlockSpec((None, TILE, 2 * D_MODEL), lat),
        out_shape=jax.ShapeDtypeStruct((BATCH, SEQ, 2 * D_MODEL), BF16),
        scratch_shapes=_SCAN_SCRATCH,
        compiler_params=pltpu.CompilerParams(
            dimension_semantics=("parallel", "arbitrary"), vmem_limit_bytes=VMEM_LIMIT),
        name="bwd_scan",
    )(q, f, v, dtx, xs, bc, dt, g, z, o_f, y_f, lb, a_pad, hg_norm, d_skip, ssd_norm)


def _out_ffn_kernel(y_ref, x_ref, mod_ref, wo_ref, nf_ref, wg_ref, wu_ref, wd_ref, fn_ref,
                    out_ref, u_s, acc_s):
    g1 = mod_ref[:, 2 * D_MODEL:3 * D_MODEL]
    sh2 = mod_ref[:, 3 * D_MODEL:4 * D_MODEL]
    sc2 = mod_ref[:, 4 * D_MODEL:5 * D_MODEL]
    g2 = mod_ref[:, 5 * D_MODEL:6 * D_MODEL]
    h = x_ref[...] + g1 * _dot(y_ref[...], wo_ref[...])
    u_s[...] = (_rms(h, nf_ref[...]) * (1.0 + sc2) + sh2).astype(BF16)
    for c0 in range(0, D_FF, FFN_CHUNK):
        gate = _dot(u_s[...], wg_ref[:, c0:c0 + FFN_CHUNK])
        up = _dot(u_s[...], wu_ref[:, c0:c0 + FFN_CHUNK])
        part = _dot((_silu(gate) * up).astype(BF16), wd_ref[c0:c0 + FFN_CHUNK, :])
        if c0 == 0:
            acc_s[...] = part
        else:
            acc_s[...] += part
    out_ref[...] = _rms(h + g2 * acc_s[...], fn_ref[...])


def _out_ffn(ymix, x, mod3, w_out, norm_ffn, w_gate, w_up, w_down, final_norm):
    tok = lambda b, t: (b, t, 0)
    return pl.pallas_call(
        _out_ffn_kernel,
        grid=(BATCH, N_LAT_TILES),
        in_specs=[
            pl.BlockSpec((None, TILE, 2 * D_MODEL), tok),
            pl.BlockSpec((None, TILE, D_MODEL), tok),
            pl.BlockSpec((None, 1, N_MOD * D_MODEL), lambda b, t: (b, 0, 0)),
            _const_spec((2 * D_MODEL, D_MODEL)),
            _const_spec((1, D_MODEL)),
            _const_spec((D_MODEL, D_FF)),
            _const_spec((D_MODEL, D_FF)),
            _const_spec((D_FF, D_MODEL)),
            _const_spec((1, D_MODEL)),
        ],
        out_specs=pl.BlockSpec((None, TILE, D_MODEL), tok),
        out_shape=jax.ShapeDtypeStruct((BATCH, SEQ, D_MODEL), F32),
        scratch_shapes=[pltpu.VMEM((TILE, D_MODEL), BF16), pltpu.VMEM((TILE, D_MODEL), F32)],
        compiler_params=pltpu.CompilerParams(
            dimension_semantics=("parallel", "arbitrary"), vmem_limit_bytes=VMEM_LIMIT),
        name="out_ffn",
    )(ymix, x, mod3, w_out, norm_ffn, w_gate, w_up, w_down, final_norm)


def kernel(x, c, ctx, c_ctx, w_ada, b_ada, norm_mix, w_in, conv_w, conv_b, ssd_a_log, ssd_dt_bias,
           ssd_d, ssd_norm, hgrn_lb_raw, hgrn_norm, w_out, norm_ffn, w_gate, w_up, w_down, final_norm):
    assert x.shape == (BATCH, SEQ, D_MODEL) and ctx.shape == (BATCH, CTX_LEN, D_MODEL)
    assert w_ada.shape == (1, D_MODEL, N_MOD * D_MODEL) and w_in.shape == (1, D_MODEL, D_IN)

    c_all = jnp.concatenate([c, c_ctx[None, :], jnp.zeros((MOD_ROWS - BATCH - 1, D_MODEL), F32)], axis=0)
    w_in_b = jnp.pad(w_in[0].astype(BF16), ((0, 0), (0, D_IN_PAD - D_IN)))
    dt_bias = jnp.pad(ssd_dt_bias[0].reshape(1, 2 * SSD_HEADS), ((0, 0), (0, LANES - 2 * SSD_HEADS)))
    a_pad = jnp.pad(-jnp.exp(ssd_a_log[0].astype(F32)).reshape(1, 2 * SSD_HEADS),
                    ((0, 0), (0, LANES - 2 * SSD_HEADS)))
    lbs = jnp.cumsum(jax.nn.softmax(hgrn_lb_raw.astype(F32), axis=0), axis=0)[0][:, None, :]
    d_skip = jnp.repeat(ssd_d[0], SSD_HEADDIM)[None, :]

    mod = _ada(c_all, w_ada[0], b_ada)
    mod3 = mod.reshape(MOD_ROWS, 1, N_MOD * D_MODEL)
    q, f, v, g, z, xs, dtx_f, dtx_b, bc, dt = _in_proj(
        x, ctx, mod3, norm_mix, w_in_b, conv_w[0], conv_b, dt_bias)
    o_f, y_f = _fwd_scan(q, f, v, dtx_f, bc, dt, lbs, a_pad)
    ymix = _bwd_scan(q, f, v, dtx_b, xs, bc, dt, g, z, o_f, y_f, lbs, a_pad,
                     hgrn_norm, d_skip, ssd_norm)
    return _out_ffn(ymix, x, mod3, w_out[0].astype(BF16), norm_ffn, w_gate[0].astype(BF16),
                    w_up[0].astype(BF16), w_down[0].astype(BF16), final_norm[None, :])
```

```python
import jax
import jax.numpy as jnp
from jax import lax
from jax.experimental import pallas as pl
from jax.experimental.pallas import tpu as pltpu

F32 = jnp.float32
BF16 = jnp.bfloat16

D_MODEL = 1024
BATCH = 8
SEQ = 4096
CTX_LEN = 256
GRID_W = 64
HG_HEADS = 8
HG_F = 128
HG_CHUNK = 64
SSD_HEADS = 16
SSD_HEADDIM = 64
SSD_GROUPS = 4
SSD_HPG = SSD_HEADS // SSD_GROUPS
SSD_STATE = 128
SSD_CONV = 5
SSD_CHUNK = 128
SSD_WIDTH = SSD_HEADS * SSD_HEADDIM
SSD_BC = SSD_GROUPS * SSD_STATE
D_FF = 2816
N_MOD = 6
EPS = 1e-6
LOG2E = 1.4426950408889634

LANES = 128
SUBLANES = 8

TILE = 256
N_LAT_TILES = SEQ // TILE
N_TILES = N_LAT_TILES + 1
MOD_ROWS = 16
CTX_ROW = BATCH
HG_NC = TILE // HG_CHUNK
SSD_NC = TILE // SSD_CHUNK
HG_PAIR = 2 * HG_F
SSD_GRP_W = SSD_HPG * SSD_HEADDIM

COL_Q = 0
COL_F = 1024
COL_I = 3072
COL_G = 4096
COL_Z = 5120
COL_XBC = 6144
COL_DT = 8192
D_IN = 8224
D_IN_PAD = COL_DT + LANES
PROJ_CHUNK = 512
CONV_HALO = SUBLANES
FFN_CHUNK = 256
FFN_TILE = 2 * TILE

VMEM_LIMIT = 56 * 1024 * 1024


def _silu(x):
    return x * jax.nn.sigmoid(x)


def _rms(x, w):
    return x * lax.rsqrt(jnp.mean(x * x, axis=-1, keepdims=True) + EPS) * w


def _dot(a, b):
    return jnp.dot(a, b, preferred_element_type=F32)


def _dot_nt(a, b):
    return lax.dot_general(a, b, (((1,), (1,)), ((), ())), preferred_element_type=F32)


def _dot_tn(a, b):
    return lax.dot_general(a, b, (((0,), (0,)), ((), ())), preferred_element_type=F32)


def _cumsum_mm(tri_b, x):
    hi = x.astype(BF16)
    r1 = x - hi.astype(F32)
    mid = r1.astype(BF16)
    lo = (r1 - mid.astype(F32)).astype(BF16)
    return _dot(tri_b, hi) + _dot(tri_b, mid) + _dot(tri_b, lo)


def _block_diag(a, b):
    z = jnp.zeros_like(a)
    return jnp.concatenate([jnp.concatenate([a, z], axis=1), jnp.concatenate([z, b], axis=1)], axis=0)


def _lane_pair(col_a, col_b):
    n = col_a.shape[0]
    lane = lax.broadcasted_iota(jnp.int32, (n, LANES), 1)
    return jnp.where(lane < SSD_HEADDIM, col_a, col_b)


def _const_spec(shape):
    nd = len(shape)
    return pl.BlockSpec(shape, lambda *_: (0,) * nd, pipeline_mode=pl.Buffered(1))


def _ada_kernel(c_ref, w_ref, b_ref, o_ref):
    act = _silu(c_ref[...]).astype(BF16)
    o_ref[...] = _dot(act, w_ref[...].astype(BF16)) + b_ref[...]


def _ada(c_all, w_ada, b_ada):
    n = N_MOD * D_MODEL
    bn = 1536
    return pl.pallas_call(
        _ada_kernel,
        grid=(n // bn,),
        in_specs=[
            pl.BlockSpec((MOD_ROWS, D_MODEL), lambda j: (0, 0)),
            pl.BlockSpec((D_MODEL, bn), lambda j: (0, j)),
            pl.BlockSpec((1, bn), lambda j: (0, j)),
        ],
        out_specs=pl.BlockSpec((MOD_ROWS, bn), lambda j: (0, j)),
        out_shape=jax.ShapeDtypeStruct((MOD_ROWS, n), F32),
        compiler_params=pltpu.CompilerParams(vmem_limit_bytes=VMEM_LIMIT),
        name="ada",
    )(c_all, w_ada, b_ada)


def _in_proj_kernel(x_ref, ctx_ref, mod_ref, nw_ref, w_ref, cw_ref, cb_ref, dtb_ref,
                    q_ref, f_ref, v_ref, g_ref, z_ref, xs_ref, dxf_ref, dxb_ref, bc_ref, dt_ref,
                    u_s, xbc_s):
    t = pl.program_id(1)
    is_ctx = t == 0
    x = jnp.where(is_ctx, ctx_ref[...], x_ref[...])
    shift = mod_ref[:, 0:D_MODEL]
    scale = mod_ref[:, D_MODEL:2 * D_MODEL]
    u_s[...] = (_rms(x, nw_ref[...]) * (1.0 + scale) + shift).astype(BF16)

    def proj(c0, width):
        return _dot(u_s[...], w_ref[:, c0:c0 + width])

    dt = jax.nn.softplus(proj(COL_DT, LANES) + dtb_ref[...])
    dt_ref[...] = dt
    zero_halo = jnp.zeros((CONV_HALO, SSD_WIDTH + 2 * SSD_BC), F32)
    xbc_s[0:CONV_HALO, :] = zero_halo
    xbc_s[CONV_HALO + TILE:, :] = zero_halo
    for c0 in range(COL_XBC, COL_DT, PROJ_CHUNK):
        xbc_s[CONV_HALO:CONV_HALO + TILE, c0 - COL_XBC:c0 - COL_XBC + PROJ_CHUNK] = proj(c0, PROJ_CHUNK)

    def proj_chunk(c0):
        y = proj(c0, PROJ_CHUNK)
        if c0 < COL_F:
            q_ref[:, c0 - COL_Q:c0 - COL_Q + PROJ_CHUNK] = _silu(y) * (HG_F ** -0.5)
        elif c0 < COL_I:
            f_ref[:, c0 - COL_F:c0 - COL_F + PROJ_CHUNK] = y
        elif c0 < COL_G:
            v_ref[:, c0 - COL_I:c0 - COL_I + PROJ_CHUNK] = y.astype(BF16)
        elif c0 < COL_Z:
            g_ref[:, c0 - COL_G:c0 - COL_G + PROJ_CHUNK] = _silu(y)
        else:
            z_ref[:, c0 - COL_Z:c0 - COL_Z + PROJ_CHUNK] = _silu(y)

    row_len = jnp.where(is_ctx, CTX_LEN, GRID_W)
    pos = lax.broadcasted_iota(jnp.int32, (TILE, LANES), 0) & (row_len - 1)
    pad = SSD_CONV // 2
    valid = [jnp.where((pos + (k - pad) >= 0) & (pos + (k - pad) < row_len), 1.0, 0.0)
             for k in range(SSD_CONV)]
    n_rows = TILE + 2 * CONV_HALO

    def conv_slab(j):
        cols = slice(j * LANES, (j + 1) * LANES)
        slab = xbc_s[:, cols]
        acc = jnp.broadcast_to(cb_ref[:, cols], (TILE, LANES))
        for k in range(SSD_CONV):
            win = slab if k == pad else pltpu.roll(slab, (pad - k) % n_rows, axis=0)
            acc = acc + (cw_ref[k:k + 1, cols] * valid[k]) * win[CONV_HALO:CONV_HALO + TILE]
        act = _silu(acc)
        if j < SSD_WIDTH // LANES:
            xs_ref[:, cols] = act
            h0 = 2 * j
            dxf_ref[:, cols] = (_lane_pair(dt[:, h0:h0 + 1], dt[:, h0 + 1:h0 + 2]) * act).astype(BF16)
            hb = SSD_HEADS + h0
            dxb_ref[:, cols] = (_lane_pair(dt[:, hb:hb + 1], dt[:, hb + 1:hb + 2]) * act).astype(BF16)
        else:
            bc_ref[:, j * LANES - SSD_WIDTH:(j + 1) * LANES - SSD_WIDTH] = act.astype(BF16)

    for j in range((SSD_WIDTH + 2 * SSD_BC) // LANES):
        conv_slab(j)
    for c0 in range(COL_Q, COL_XBC, PROJ_CHUNK):
        proj_chunk(c0)


def _in_proj(x, ctx, mod3, norm_w, w_in, conv_w, conv_b, dt_bias):
    lat = lambda b, t: (b, jnp.maximum(t - 1, 0), 0)
    tok = lambda b, t: (b, t, 0)
    n_tok = N_TILES * TILE

    def out(width, dtype):
        return jax.ShapeDtypeStruct((BATCH, n_tok, width), dtype), pl.BlockSpec((None, TILE, width), tok)

    outs = [out(D_MODEL, F32), out(2 * D_MODEL, F32), out(D_MODEL, BF16), out(D_MODEL, F32),
            out(D_MODEL, F32), out(SSD_WIDTH, F32), out(SSD_WIDTH, BF16), out(SSD_WIDTH, BF16),
            out(2 * SSD_BC, BF16), out(LANES, F32)]
    return pl.pallas_call(
        _in_proj_kernel,
        grid=(BATCH, N_TILES),
        in_specs=[
            pl.BlockSpec((None, TILE, D_MODEL), lat),
            pl.BlockSpec((None, CTX_LEN, D_MODEL), lambda b, t: (b, 0, 0)),
            pl.BlockSpec((None, 1, N_MOD * D_MODEL), lambda b, t: (jnp.where(t == 0, CTX_ROW, b), 0, 0)),
            _const_spec((1, D_MODEL)),
            _const_spec((D_MODEL, D_IN_PAD)),
            _const_spec((SSD_CONV, SSD_WIDTH + 2 * SSD_BC)),
            _const_spec((1, SSD_WIDTH + 2 * SSD_BC)),
            _const_spec((1, LANES)),
        ],
        out_specs=[o[1] for o in outs],
        out_shape=[o[0] for o in outs],
        scratch_shapes=[
            pltpu.VMEM((TILE, D_MODEL), BF16),
            pltpu.VMEM((TILE + 2 * CONV_HALO, SSD_WIDTH + 2 * SSD_BC), F32),
        ],
        compiler_params=pltpu.CompilerParams(
            dimension_semantics=("parallel", "arbitrary"), vmem_limit_bytes=VMEM_LIMIT),
        name="in_proj",
    )(x, ctx, mod3, norm_w, w_in, conv_w, conv_b, dt_bias)


def _tri(n, reverse):
    i = lax.broadcasted_iota(jnp.int32, (n, n), 0)
    j = lax.broadcasted_iota(jnp.int32, (n, n), 1)
    return (j >= i) if reverse else (j <= i)


def _hgrn_stages(q_ref, f_ref, v_ref, lb_ref, s_ref, qd_s, ki_s, ke_s, sc_s, sp_s, et_s, reverse, emit):
    mask = _tri(HG_CHUNK, reverse)
    tri_b = jnp.where(mask, 1.0, 0.0).astype(BF16)
    i2 = lax.broadcasted_iota(jnp.int32, (HG_CHUNK, LANES), 0)
    j2 = lax.broadcasted_iota(jnp.int32, (HG_CHUNK, LANES), 1) & (HG_CHUNK - 1)
    mask2 = (j2 >= i2) if reverse else (j2 <= i2)
    last = slice(0, 1) if reverse else slice(HG_CHUNK - 1, HG_CHUNK)
    chunk_rows = [slice(c * HG_CHUNK, (c + 1) * HG_CHUNK) for c in range(HG_NC)]
    pair_cols = [slice(p * HG_PAIR, (p + 1) * HG_PAIR) for p in range(HG_HEADS // 2)]
    order = list(reversed(range(HG_NC))) if reverse else list(range(HG_NC))

    def gates():
        for c in range(HG_NC):
            rows = chunk_rows[c]
            for cols in pair_cols:
                lb = lb_ref[:, cols]
                f = lb + (1.0 - lb) * jax.nn.sigmoid(f_ref[rows, cols])
                k = 1.0 - f
                cum = _cumsum_mm(tri_b, jnp.log(f)) * LOG2E
                e_tot = jnp.exp2(cum[last, :])
                k_inv = k * jnp.exp2(-cum)
                qd_s[rows, cols] = (q_ref[rows, cols] * jnp.exp2(cum)).astype(BF16)
                ki_s[rows, cols] = k_inv.astype(BF16)
                ke_s[rows, cols] = (k_inv * e_tot).astype(BF16)
                et_s[c * SUBLANES:c * SUBLANES + 1, cols] = e_tot

    def scores():
        for c in range(HG_NC):
            rows = chunk_rows[c]
            for p, cols in enumerate(pair_cols):
                kp = ki_s[rows, cols]
                sc = _dot_nt(qd_s[rows, cols], _block_diag(kp[:, :HG_F], kp[:, HG_F:]))
                sc_s[rows, p * LANES:(p + 1) * LANES] = jnp.where(mask2, sc, 0.0).astype(BF16)

    def states():
        for h in range(HG_HEADS):
            cols = slice(h * HG_F, (h + 1) * HG_F)
            s = s_ref[h]
            for c in order:
                rows = chunk_rows[c]
                sp_s[c, h] = s.astype(BF16)
                s = s * et_s[c * SUBLANES:c * SUBLANES + 1, cols] + _dot_tn(v_ref[rows, cols], ke_s[rows, cols])
            s_ref[h] = s

    def outputs():
        for c in range(HG_NC):
            rows = chunk_rows[c]
            for p, cols in enumerate(pair_cols):
                vp = v_ref[rows, cols]
                o = _dot(sc_s[rows, p * LANES:(p + 1) * LANES], _block_diag(vp[:, :HG_F], vp[:, HG_F:]))
                o = o + _dot_nt(qd_s[rows, cols], _block_diag(sp_s[c, 2 * p], sp_s[c, 2 * p + 1]))
                emit(rows, p, o)

    return [gates, scores, states, outputs]


def _ssd_stages(dtx_ref, bc_ref, dt_ref, a_ref, s_ref, cum_s, cumt_s, sc_s, m_s, dxe_s, ecum_s, sp_s,
                reverse, dcol, emit):
    mask = _tri(SSD_CHUNK, reverse)
    tri_b = jnp.where(mask, 1.0, 0.0).astype(BF16)
    last = slice(0, 1) if reverse else slice(SSD_CHUNK - 1, SSD_CHUNK)
    chunk_rows = [slice(c * SSD_CHUNK, (c + 1) * SSD_CHUNK) for c in range(SSD_NC)]
    order = list(reversed(range(SSD_NC))) if reverse else list(range(SSD_NC))
    lane = lax.broadcasted_iota(jnp.int32, (SSD_CHUNK, LANES), 1)
    low = lane < SSD_HEADDIM

    def decays():
        for c in range(SSD_NC):
            cum = _cumsum_mm(tri_b, dt_ref[chunk_rows[c], :] * (a_ref[...] * LOG2E))
            cum_s[c] = cum
            cumt_s[c] = cum.T

    def scores():
        for c in range(SSD_NC):
            rows = chunk_rows[c]
            for g in range(SSD_GROUPS):
                bg = bc_ref[rows, g * SSD_STATE:(g + 1) * SSD_STATE]
                cg = bc_ref[rows, SSD_BC + g * SSD_STATE:SSD_BC + (g + 1) * SSD_STATE]
                sc_s[c, g] = _dot_nt(cg, bg)

    def masks():
        for c in range(SSD_NC):
            rows = chunk_rows[c]
            cum = cum_s[c]
            cum_t = cumt_s[c]
            for pair in range(SSD_HEADS // 2):
                e_cum, e_end = [], []
                for h in (2 * pair, 2 * pair + 1):
                    col = dcol + h
                    r = jnp.broadcast_to(cum[:, col:col + 1], (SSD_CHUNK, SSD_CHUNK))
                    row_t = cum_t[col:col + 1, :]
                    decay = jnp.exp2(jnp.where(mask, r - row_t, -jnp.inf))
                    m_s[c, h] = (sc_s[c, h // SSD_HPG] * decay).astype(BF16)
                    e_cum.append(jnp.exp2(r))
                    e_end.append(jnp.exp2(row_t[:, last.start:last.stop] - r))
                cols = slice(pair * LANES, (pair + 1) * LANES)
                ecum_s[rows, cols] = jnp.where(low, e_cum[0], e_cum[1])
                dxe_s[rows, cols] = (jnp.where(low, e_end[0], e_end[1])
                                     * dtx_ref[rows, cols].astype(F32)).astype(BF16)

    def states():
        for g in range(SSD_GROUPS):
            grp = slice(g * SSD_GRP_W, (g + 1) * SSD_GRP_W)
            s = s_ref[grp, :]
            for c in order:
                rows = chunk_rows[c]
                sp_s[c, grp, :] = s.astype(BF16)
                ds = _dot_tn(dxe_s[rows, grp], bc_ref[rows, g * SSD_STATE:(g + 1) * SSD_STATE])
                tot_t = cumt_s[c][:, last.start:last.stop]
                dec = jnp.concatenate(
                    [jnp.broadcast_to(jnp.exp2(tot_t[dcol + g * SSD_HPG + hh:dcol + g * SSD_HPG + hh + 1, :]),
                                      (SSD_HEADDIM, SSD_STATE)) for hh in range(SSD_HPG)], axis=0)
                s = s * dec + ds
            s_ref[grp, :] = s

    def outputs():
        for c in range(SSD_NC):
            rows = chunk_rows[c]
            for g in range(SSD_GROUPS):
                grp = slice(g * SSD_GRP_W, (g + 1) * SSD_GRP_W)
                cg = bc_ref[rows, SSD_BC + g * SSD_STATE:SSD_BC + (g + 1) * SSD_STATE]
                y_inter = _dot_nt(cg, sp_s[c, grp, :])
                ys = []
                for pp in range(SSD_HPG // 2):
                    pair = g * (SSD_HPG // 2) + pp
                    cols = slice(pair * LANES, (pair + 1) * LANES)
                    dp = dtx_ref[rows, cols]
                    zero = jnp.zeros_like(dp)
                    rhs = jnp.concatenate([jnp.where(low, dp, zero), jnp.where(low, zero, dp)], axis=0)
                    y_intra = _dot(jnp.concatenate([m_s[c, 2 * pair], m_s[c, 2 * pair + 1]], axis=1), rhs)
                    ys.append(y_intra + ecum_s[rows, cols] * y_inter[:, pp * LANES:(pp + 1) * LANES])
                emit(rows, g, ys)

    return [decays, scores, masks, states, outputs]


def _run_scan_stages(hg, sd):
    for stage in (sd[0], hg[0], sd[1], hg[1], sd[2], hg[2], sd[3], hg[3], sd[4]):
        stage()


_SCAN_SCRATCH = [
    pltpu.VMEM((HG_HEADS, HG_F, HG_F), F32),
    pltpu.VMEM((SSD_WIDTH, SSD_STATE), F32),
    pltpu.VMEM((TILE, D_MODEL), BF16),
    pltpu.VMEM((TILE, D_MODEL), BF16),
    pltpu.VMEM((TILE, D_MODEL), BF16),
    pltpu.VMEM((TILE, HG_HEADS // 2 * LANES), BF16),
    pltpu.VMEM((HG_NC, HG_HEADS, HG_F, HG_F), BF16),
    pltpu.VMEM((HG_NC * SUBLANES, D_MODEL), F32),
    pltpu.VMEM((SSD_NC, SSD_CHUNK, LANES), F32),
    pltpu.VMEM((SSD_NC, LANES, SSD_CHUNK), F32),
    pltpu.VMEM((SSD_NC, SSD_GROUPS, SSD_CHUNK, SSD_CHUNK), F32),
    pltpu.VMEM((SSD_NC, SSD_HEADS, SSD_CHUNK, SSD_CHUNK), BF16),
    pltpu.VMEM((TILE, SSD_WIDTH), BF16),
    pltpu.VMEM((TILE, SSD_WIDTH), F32),
    pltpu.VMEM((SSD_NC, SSD_WIDTH, SSD_STATE), BF16),
]


def _fwd_scan_kernel(q_ref, f_ref, v_ref, dtx_ref, bc_ref, dt_ref, lb_ref, a_ref,
                     o_ref, y_ref, shg_s, sssd_s, qd_s, ki_s, ke_s, hsc_s, hsp_s, et_s,
                     cum_s, cumt_s, ssc_s, m_s, dxe_s, ecum_s, ssp_s):
    t = pl.program_id(1)

    @pl.when(t == 0)
    def _():
        shg_s[...] = jnp.zeros_like(shg_s)
        sssd_s[...] = jnp.zeros_like(sssd_s)

    def emit_hg(rows, p, o):
        o_ref[rows, p * HG_PAIR:(p + 1) * HG_PAIR] = o

    def emit_ssd(rows, g, ys):
        for i, y in enumerate(ys):
            y_ref[rows, g * SSD_GRP_W + i * LANES:g * SSD_GRP_W + (i + 1) * LANES] = y

    hg = _hgrn_stages(q_ref, f_ref, v_ref, lb_ref, shg_s, qd_s, ki_s, ke_s, hsc_s, hsp_s, et_s, False, emit_hg)
    sd = _ssd_stages(dtx_ref, bc_ref, dt_ref, a_ref, sssd_s, cum_s, cumt_s, ssc_s, m_s, dxe_s, ecum_s, ssp_s,
                     False, 0, emit_ssd)
    _run_scan_stages(hg, sd)


def _fwd_scan(q, f, v, dtx, bc, dt, lb, a_pad):
    tok = lambda b, t: (b, t, 0)
    lat = lambda b, t: (b, jnp.maximum(t - 1, 0), 0)
    return pl.pallas_call(
        _fwd_scan_kernel,
        grid=(BATCH, N_TILES),
        in_specs=[
            pl.BlockSpec((None, TILE, D_MODEL), tok),
            pl.BlockSpec((None, TILE, D_MODEL), tok),
            pl.BlockSpec((None, TILE, D_MODEL), tok),
            pl.BlockSpec((None, TILE, SSD_WIDTH), tok),
            pl.BlockSpec((None, TILE, 2 * SSD_BC), tok),
            pl.BlockSpec((None, TILE, LANES), tok),
            pl.BlockSpec((None, 1, D_MODEL), lambda b, t: (0, 0, 0)),
            _const_spec((1, LANES)),
        ],
        out_specs=[pl.BlockSpec((None, TILE, D_MODEL), lat), pl.BlockSpec((None, TILE, SSD_WIDTH), lat)],
        out_shape=[jax.ShapeDtypeStruct((BATCH, SEQ, D_MODEL), F32),
                   jax.ShapeDtypeStruct((BATCH, SEQ, SSD_WIDTH), F32)],
        scratch_shapes=_SCAN_SCRATCH,
        compiler_params=pltpu.CompilerParams(
            dimension_semantics=("parallel", "arbitrary"), vmem_limit_bytes=VMEM_LIMIT),
        name="fwd_scan",
    )(q, f, v, dtx, bc, dt, lb, a_pad)


def _bwd_scan_kernel(q_ref, f_ref, v_ref, dtx_ref, xs_ref, bc_ref, dt_ref, g_ref, z_ref, of_ref, yf_ref,
                     lb_ref, a_ref, hnw_ref, dsk_ref, snw_ref, out_ref,
                     shg_s, sssd_s, qd_s, ki_s, ke_s, hsc_s, hsp_s, et_s,
                     cum_s, cumt_s, ssc_s, m_s, dxe_s, ecum_s, ssp_s):
    s = pl.program_id(1)

    @pl.when(s == 0)
    def _():
        shg_s[...] = jnp.zeros_like(shg_s)
        sssd_s[...] = jnp.zeros_like(sssd_s)

    def emit_hg(rows, p, o_b):
        for i in range(2):
            cols = slice(p * HG_PAIR + i * HG_F, p * HG_PAIR + (i + 1) * HG_F)
            o = of_ref[rows, cols] + o_b[:, i * HG_F:(i + 1) * HG_F]
            out_ref[rows, cols] = (_rms(o, hnw_ref[...]) * g_ref[rows, cols]).astype(BF16)

    def emit_ssd(rows, g, ys):
        us = []
        for i, y_b in enumerate(ys):
            cols = slice(g * SSD_GRP_W + i * LANES, g * SSD_GRP_W + (i + 1) * LANES)
            y = yf_ref[rows, cols] + y_b + dsk_ref[:, cols] * xs_ref[rows, cols]
            us.append(y * z_ref[rows, cols])
        ms = sum(jnp.sum(u * u, axis=-1, keepdims=True) for u in us) / SSD_GRP_W
        inv = lax.rsqrt(ms + EPS)
        for i, u in enumerate(us):
            cols = slice(g * SSD_GRP_W + i * LANES, g * SSD_GRP_W + (i + 1) * LANES)
            out_ref[rows, SSD_WIDTH + cols.start:SSD_WIDTH + cols.stop] = (
                u * inv * snw_ref[:, cols]).astype(BF16)

    hg = _hgrn_stages(q_ref, f_ref, v_ref, lb_ref, shg_s, qd_s, ki_s, ke_s, hsc_s, hsp_s, et_s, True, emit_hg)
    sd = _ssd_stages(dtx_ref, bc_ref, dt_ref, a_ref, sssd_s, cum_s, cumt_s, ssc_s, m_s, dxe_s, ecum_s, ssp_s,
                     True, SSD_HEADS, emit_ssd)
    _run_scan_stages(hg, sd)


def _bwd_scan(q, f, v, dtx, xs, bc, dt, g, z, o_f, y_f, lb, a_pad, hg_norm, d_skip, ssd_norm):
    tok = lambda b, s: (b, jnp.where(s == 0, 0, N_TILES - s), 0)
    fcol = lambda b, s: (b, jnp.where(s == 0, 0, N_TILES - s), 1)
    lat = lambda b, s: (b, jnp.where(s == 0, N_LAT_TILES - 1, N_LAT_TILES - s), 0)
    return pl.pallas_call(
        _bwd_scan_kernel,
        grid=(BATCH, N_TILES),
        in_specs=[
            pl.BlockSpec((None, TILE, D_MODEL), tok),
            pl.BlockSpec((None, TILE, D_MODEL), fcol),
            pl.BlockSpec((None, TILE, D_MODEL), tok),
            pl.BlockSpec((None, TILE, SSD_WIDTH), tok),
            pl.BlockSpec((None, TILE, SSD_WIDTH), tok),
            pl.BlockSpec((None, TILE, 2 * SSD_BC), tok),
            pl.BlockSpec((None, TILE, LANES), tok),
            pl.BlockSpec((None, TILE, D_MODEL), tok),
            pl.BlockSpec((None, TILE, SSD_WIDTH), tok),
            pl.BlockSpec((None, TILE, D_MODEL), lat),
            pl.BlockSpec((None, TILE, SSD_WIDTH), lat),
            pl.BlockSpec((None, 1, D_MODEL), lambda b, s: (1, 0, 0)),
            _const_spec((1, LANES)),
            _const_spec((1, HG_F)),
            _const_spec((1, SSD_WIDTH)),
            _const_spec((1, SSD_WIDTH)),
        ],
        out_specs=pl.BlockSpec((None, TILE, 2 * D_MODEL), lat),
        out_shape=jax.ShapeDtypeStruct((BATCH, SEQ, 2 * D_MODEL), BF16),
        scratch_shapes=_SCAN_SCRATCH,
        compiler_params=pltpu.CompilerParams(
            dimension_semantics=("parallel", "arbitrary"), vmem_limit_bytes=VMEM_LIMIT),
        name="bwd_scan",
    )(q, f, v, dtx, xs, bc, dt, g, z, o_f, y_f, lb, a_pad, hg_norm, d_skip, ssd_norm)


def _out_ffn_kernel(y_ref, x_ref, mod_ref, wo_ref, nf_ref, wg_ref, wu_ref, wd_ref, fn_ref,
                    out_ref, u_s, h_s, acc_s):
    g1 = mod_ref[:, 2 * D_MODEL:3 * D_MODEL]
    sh2 = mod_ref[:, 3 * D_MODEL:4 * D_MODEL]
    sc2 = mod_ref[:, 4 * D_MODEL:5 * D_MODEL]
    g2 = mod_ref[:, 5 * D_MODEL:6 * D_MODEL]
    halves = [slice(i * TILE, (i + 1) * TILE) for i in range(FFN_TILE // TILE)]
    n_chunks = D_FF // FFN_CHUNK

    for r in halves:
        h = x_ref[r, :] + g1 * _dot(y_ref[r, :], wo_ref[...])
        h_s[r, :] = h
        u_s[r, :] = (_rms(h, nf_ref[...]) * (1.0 + sc2) + sh2).astype(BF16)

    for r in halves:
        def gate_up(c):
            cols = slice(c * FFN_CHUNK, (c + 1) * FFN_CHUNK)
            return _dot(u_s[r, :], wg_ref[:, cols]), _dot(u_s[r, :], wu_ref[:, cols])

        cur = gate_up(0)
        for c in range(n_chunks):
            nxt = gate_up(c + 1) if c + 1 < n_chunks else None
            part = _dot((_silu(cur[0]) * cur[1]).astype(BF16), wd_ref[c * FFN_CHUNK:(c + 1) * FFN_CHUNK, :])
            if c == 0:
                acc_s[r, :] = part
            else:
                acc_s[r, :] += part
            cur = nxt
        out_ref[r, :] = _rms(h_s[r, :] + g2 * acc_s[r, :], fn_ref[...])


def _out_ffn(ymix, x, mod3, w_out, norm_ffn, w_gate, w_up, w_down, final_norm):
    tok = lambda b, t: (b, t, 0)
    return pl.pallas_call(
        _out_ffn_kernel,
        grid=(BATCH, SEQ // FFN_TILE),
        in_specs=[
            pl.BlockSpec((None, FFN_TILE, 2 * D_MODEL), tok),
            pl.BlockSpec((None, FFN_TILE, D_MODEL), tok),
            pl.BlockSpec((None, 1, N_MOD * D_MODEL), lambda b, t: (b, 0, 0)),
            _const_spec((2 * D_MODEL, D_MODEL)),
            _const_spec((1, D_MODEL)),
            _const_spec((D_MODEL, D_FF)),
            _const_spec((D_MODEL, D_FF)),
            _const_spec((D_FF, D_MODEL)),
            _const_spec((1, D_MODEL)),
        ],
        out_specs=pl.BlockSpec((None, FFN_TILE, D_MODEL), tok),
        out_shape=jax.ShapeDtypeStruct((BATCH, SEQ, D_MODEL), F32),
        scratch_shapes=[pltpu.VMEM((FFN_TILE, D_MODEL), BF16), pltpu.VMEM((FFN_TILE, D_MODEL), F32),
                        pltpu.VMEM((FFN_TILE, D_MODEL), F32)],
        compiler_params=pltpu.CompilerParams(
            dimension_semantics=("parallel", "arbitrary"), vmem_limit_bytes=VMEM_LIMIT),
        name="out_ffn",
    )(ymix, x, mod3, w_out, norm_ffn, w_gate, w_up, w_down, final_norm)


def kernel(x, c, ctx, c_ctx, w_ada, b_ada, norm_mix, w_in, conv_w, conv_b, ssd_a_log, ssd_dt_bias,
           ssd_d, ssd_norm, hgrn_lb_raw, hgrn_norm, w_out, norm_ffn, w_gate, w_up, w_down, final_norm):
    assert x.shape == (BATCH, SEQ, D_MODEL) and ctx.shape == (BATCH, CTX_LEN, D_MODEL)
    assert w_ada.shape == (1, D_MODEL, N_MOD * D_MODEL) and w_in.shape == (1, D_MODEL, D_IN)

    c_all = jnp.concatenate([c, c_ctx[None, :], jnp.zeros((MOD_ROWS - BATCH - 1, D_MODEL), F32)], axis=0)
    w_in_b = jnp.pad(w_in[0].astype(BF16), ((0, 0), (0, D_IN_PAD - D_IN)))
    dt_bias = jnp.pad(ssd_dt_bias[0].reshape(1, 2 * SSD_HEADS), ((0, 0), (0, LANES - 2 * SSD_HEADS)))
    a_pad = jnp.pad(-jnp.exp(ssd_a_log[0].astype(F32)).reshape(1, 2 * SSD_HEADS),
                    ((0, 0), (0, LANES - 2 * SSD_HEADS)))
    lbs = jnp.cumsum(jax.nn.softmax(hgrn_lb_raw.astype(F32), axis=0), axis=0)[0][:, None, :]
    d_skip = jnp.repeat(ssd_d[0], SSD_HEADDIM)[None, :]

    mod = _ada(c_all, w_ada[0], b_ada)
    mod3 = mod.reshape(MOD_ROWS, 1, N_MOD * D_MODEL)
    q, f, v, g, z, xs, dtx_f, dtx_b, bc, dt = _in_proj(
        x, ctx, mod3, norm_mix, w_in_b, conv_w[0], conv_b, dt_bias)
    o_f, y_f = _fwd_scan(q, f, v, dtx_f, bc, dt, lbs, a_pad)
    ymix = _bwd_scan(q, f, v, dtx_b, xs, bc, dt, g, z, o_f, y_f, lbs, a_pad,
                     hgrn_norm, d_skip, ssd_norm)
    return _out_ffn(ymix, x, mod3, w_out[0].astype(BF16), norm_ffn, w_gate[0].astype(BF16),
                    w_up[0].astype(BF16), w_down[0].astype(BF16), final_norm[None, :])
```

```python
import jax
import jax.numpy as jnp
from jax import lax
from jax.experimental import pallas as pl
from jax.experimental.pallas import tpu as pltpu

F32 = jnp.float32
BF16 = jnp.bfloat16

D_MODEL = 1024
BATCH = 8
SEQ = 4096
CTX_LEN = 256
GRID_W = 64
HG_HEADS = 8
HG_F = 128
HG_CHUNK = 64
SSD_HEADS = 16
SSD_HEADDIM = 64
SSD_GROUPS = 4
SSD_HPG = SSD_HEADS // SSD_GROUPS
SSD_STATE = 128
SSD_CONV = 5
SSD_CHUNK = 128
SSD_WIDTH = SSD_HEADS * SSD_HEADDIM
SSD_BC = SSD_GROUPS * SSD_STATE
D_FF = 2816
N_MOD = 6
EPS = 1e-6
LOG2E = 1.4426950408889634

LANES = 128
SUBLANES = 8

TILE = 256
N_LAT_TILES = SEQ // TILE
N_TILES = N_LAT_TILES + 1
MOD_ROWS = 16
CTX_ROW = BATCH
HG_NC = TILE // HG_CHUNK
SSD_NC = TILE // SSD_CHUNK
HG_PAIR = 2 * HG_F
SSD_GRP_W = SSD_HPG * SSD_HEADDIM

COL_Q = 0
COL_F = 1024
COL_I = 3072
COL_G = 4096
COL_Z = 5120
COL_XBC = 6144
COL_DT = 8192
D_IN = 8224
PROJ_CHUNK = 512
CONV_HALO = SUBLANES
FFN_CHUNK = 256
FFN_TILE = 2 * TILE

VMEM_LIMIT = 56 * 1024 * 1024


def _silu(x):
    return x * jax.nn.sigmoid(x)


def _rms(x, w):
    return x * lax.rsqrt(jnp.mean(x * x, axis=-1, keepdims=True) + EPS) * w


def _dot(a, b):
    return jnp.dot(a, b, preferred_element_type=F32)


def _dot_nt(a, b):
    return lax.dot_general(a, b, (((1,), (1,)), ((), ())), preferred_element_type=F32)


def _dot_tn(a, b):
    return lax.dot_general(a, b, (((0,), (0,)), ((), ())), preferred_element_type=F32)


def _cumsum_mm(tri_b, x):
    hi = x.astype(BF16)
    r1 = x - hi.astype(F32)
    mid = r1.astype(BF16)
    lo = (r1 - mid.astype(F32)).astype(BF16)
    return _dot(tri_b, hi) + _dot(tri_b, mid) + _dot(tri_b, lo)


def _block_diag(a, b):
    z = jnp.zeros_like(a)
    return jnp.concatenate([jnp.concatenate([a, z], axis=1), jnp.concatenate([z, b], axis=1)], axis=0)


def _lane_pair(col_a, col_b):
    n = col_a.shape[0]
    lane = lax.broadcasted_iota(jnp.int32, (n, LANES), 1)
    return jnp.where(lane < SSD_HEADDIM, col_a, col_b)


def _const_spec(shape):
    nd = len(shape)
    return pl.BlockSpec(shape, lambda *_: (0,) * nd, pipeline_mode=pl.Buffered(1))


def _ada_kernel(c_ref, w_ref, b_ref, o_ref):
    act = _silu(c_ref[...]).astype(BF16)
    o_ref[...] = _dot(act, w_ref[...].astype(BF16)) + b_ref[...]


def _ada(c_all, w_ada, b_ada):
    n = N_MOD * D_MODEL
    bn = 1536
    return pl.pallas_call(
        _ada_kernel,
        grid=(n // bn,),
        in_specs=[
            pl.BlockSpec((MOD_ROWS, D_MODEL), lambda j: (0, 0)),
            pl.BlockSpec((None, D_MODEL, bn), lambda j: (0, 0, j)),
            pl.BlockSpec((1, bn), lambda j: (0, j)),
        ],
        out_specs=pl.BlockSpec((MOD_ROWS, bn), lambda j: (0, j)),
        out_shape=jax.ShapeDtypeStruct((MOD_ROWS, n), F32),
        compiler_params=pltpu.CompilerParams(vmem_limit_bytes=VMEM_LIMIT),
        name="ada",
    )(c_all, w_ada, b_ada)


def _in_proj_kernel(x_ref, ctx_ref, mod_ref, nw_ref, w_ref, wdt_ref, cw_ref, cb_ref, dtb_ref,
                    q_ref, f_ref, v_ref, g_ref, z_ref, xs_ref, dxf_ref, dxb_ref, bc_ref, dt_ref,
                    u_s, xbc_s):
    t = pl.program_id(1)
    is_ctx = t == 0
    x = jnp.where(is_ctx, ctx_ref[...], x_ref[...])
    shift = mod_ref[:, 0:D_MODEL]
    scale = mod_ref[:, D_MODEL:2 * D_MODEL]
    u_s[...] = (_rms(x, nw_ref[...]) * (1.0 + scale) + shift).astype(BF16)

    def proj(c0, width):
        return _dot(u_s[...], w_ref[:, c0:c0 + width])

    dt = jax.nn.softplus(_dot(u_s[...], wdt_ref[...]) + dtb_ref[...])
    dt_ref[...] = dt
    zero_halo = jnp.zeros((CONV_HALO, SSD_WIDTH + 2 * SSD_BC), F32)
    xbc_s[0:CONV_HALO, :] = zero_halo
    xbc_s[CONV_HALO + TILE:, :] = zero_halo
    for c0 in range(COL_XBC, COL_DT, PROJ_CHUNK):
        xbc_s[CONV_HALO:CONV_HALO + TILE, c0 - COL_XBC:c0 - COL_XBC + PROJ_CHUNK] = proj(c0, PROJ_CHUNK)

    def proj_chunk(c0):
        y = proj(c0, PROJ_CHUNK)
        if c0 < COL_F:
            q_ref[:, c0 - COL_Q:c0 - COL_Q + PROJ_CHUNK] = (_silu(y) * (HG_F ** -0.5)).astype(BF16)
        elif c0 < COL_I:
            f_ref[:, c0 - COL_F:c0 - COL_F + PROJ_CHUNK] = y
        elif c0 < COL_G:
            v_ref[:, c0 - COL_I:c0 - COL_I + PROJ_CHUNK] = y.astype(BF16)
        elif c0 < COL_Z:
            g_ref[:, c0 - COL_G:c0 - COL_G + PROJ_CHUNK] = _silu(y).astype(BF16)
        else:
            z_ref[:, c0 - COL_Z:c0 - COL_Z + PROJ_CHUNK] = _silu(y).astype(BF16)

    row_len = jnp.where(is_ctx, CTX_LEN, GRID_W)
    pos = lax.broadcasted_iota(jnp.int32, (TILE, LANES), 0) & (row_len - 1)
    pad = SSD_CONV // 2
    valid = [jnp.where((pos + (k - pad) >= 0) & (pos + (k - pad) < row_len), 1.0, 0.0)
             for k in range(SSD_CONV)]

    def conv_slab(j):
        cols = slice(j * LANES, (j + 1) * LANES)
        blk_rows = GRID_W + 2 * CONV_HALO
        for r0 in range(0, TILE, GRID_W):
            rows = slice(r0, r0 + GRID_W)
            blk = xbc_s[r0:r0 + blk_rows, cols]
            acc = jnp.broadcast_to(cb_ref[:, cols], (GRID_W, LANES))
            for k in range(SSD_CONV):
                win = blk if k == pad else pltpu.roll(blk, (pad - k) % blk_rows, axis=0)
                acc = acc + (cw_ref[k:k + 1, cols] * valid[k][rows]) * win[CONV_HALO:CONV_HALO + GRID_W]
            act = _silu(acc)
            if j < SSD_WIDTH // LANES:
                xs_ref[rows, cols] = act.astype(BF16)
                for dref, h0 in ((dxf_ref, 2 * j), (dxb_ref, SSD_HEADS + 2 * j)):
                    dref[rows, cols] = (_lane_pair(dt[rows, h0:h0 + 1], dt[rows, h0 + 1:h0 + 2]) * act).astype(BF16)
            else:
                bc_ref[rows, j * LANES - SSD_WIDTH:(j + 1) * LANES - SSD_WIDTH] = act.astype(BF16)

    for j in range((SSD_WIDTH + 2 * SSD_BC) // LANES):
        conv_slab(j)
    for c0 in range(COL_Q, COL_XBC, PROJ_CHUNK):
        proj_chunk(c0)


def _in_proj(x, ctx, mod3, norm_w, w_in, w_dt, conv_w, conv_b, dt_bias):
    lat = lambda b, t: (b, jnp.maximum(t - 1, 0), 0)
    tok = lambda b, t: (b, t, 0)
    n_tok = N_TILES * TILE

    def out(width, dtype):
        return jax.ShapeDtypeStruct((BATCH, n_tok, width), dtype), pl.BlockSpec((None, TILE, width), tok)

    outs = [out(D_MODEL, BF16), out(2 * D_MODEL, F32), out(D_MODEL, BF16), out(D_MODEL, BF16),
            out(D_MODEL, BF16), out(SSD_WIDTH, BF16), out(SSD_WIDTH, BF16), out(SSD_WIDTH, BF16),
            out(2 * SSD_BC, BF16), out(LANES, F32)]
    return pl.pallas_call(
        _in_proj_kernel,
        grid=(BATCH, N_TILES),
        in_specs=[
            pl.BlockSpec((None, TILE, D_MODEL), lat),
            pl.BlockSpec((None, CTX_LEN, D_MODEL), lambda b, t: (b, 0, 0)),
            pl.BlockSpec((None, 1, N_MOD * D_MODEL), lambda b, t: (jnp.where(t == 0, CTX_ROW, b), 0, 0)),
            _const_spec((1, D_MODEL)),
            _const_spec((D_MODEL, D_IN)),
            _const_spec((D_MODEL, LANES)),
            _const_spec((SSD_CONV, SSD_WIDTH + 2 * SSD_BC)),
            _const_spec((1, SSD_WIDTH + 2 * SSD_BC)),
            _const_spec((1, LANES)),
        ],
        out_specs=[o[1] for o in outs],
        out_shape=[o[0] for o in outs],
        scratch_shapes=[
            pltpu.VMEM((TILE, D_MODEL), BF16),
            pltpu.VMEM((TILE + 2 * CONV_HALO, SSD_WIDTH + 2 * SSD_BC), F32),
        ],
        compiler_params=pltpu.CompilerParams(
            dimension_semantics=("parallel", "arbitrary"), vmem_limit_bytes=VMEM_LIMIT),
        name="in_proj",
    )(x, ctx, mod3, norm_w, w_in, w_dt, conv_w, conv_b, dt_bias)


def _tri(n, reverse):
    i = lax.broadcasted_iota(jnp.int32, (n, n), 0)
    j = lax.broadcasted_iota(jnp.int32, (n, n), 1)
    return (j >= i) if reverse else (j <= i)


def _hgrn_stages(q_ref, f_ref, v_ref, lb_ref, s_ref, qd_s, ki_s, ke_s, sc_s, sp_s, et_s, reverse, emit):
    mask = _tri(HG_CHUNK, reverse)
    tri_b = jnp.where(mask, 1.0, 0.0).astype(BF16)
    i2 = lax.broadcasted_iota(jnp.int32, (HG_CHUNK, LANES), 0)
    j2 = lax.broadcasted_iota(jnp.int32, (HG_CHUNK, LANES), 1) & (HG_CHUNK - 1)
    mask2 = (j2 >= i2) if reverse else (j2 <= i2)
    last = slice(0, 1) if reverse else slice(HG_CHUNK - 1, HG_CHUNK)
    chunk_rows = [slice(c * HG_CHUNK, (c + 1) * HG_CHUNK) for c in range(HG_NC)]
    pair_cols = [slice(p * HG_PAIR, (p + 1) * HG_PAIR) for p in range(HG_HEADS // 2)]
    order = list(reversed(range(HG_NC))) if reverse else list(range(HG_NC))

    def gates():
        for c in range(HG_NC):
            rows = chunk_rows[c]
            for cols in pair_cols:
                lb = lb_ref[:, cols]
                f = lb + (1.0 - lb) * jax.nn.sigmoid(f_ref[rows, cols])
                k = 1.0 - f
                cum = _cumsum_mm(tri_b, jnp.log(f)) * LOG2E
                e_tot = jnp.exp2(cum[last, :])
                k_inv = k * jnp.exp2(-cum)
                qd_s[rows, cols] = (q_ref[rows, cols] * jnp.exp2(cum)).astype(BF16)
                ki_s[rows, cols] = k_inv.astype(BF16)
                ke_s[rows, cols] = (k_inv * e_tot).astype(BF16)
                et_s[c * SUBLANES:c * SUBLANES + 1, cols] = e_tot

    def scores():
        for c in range(HG_NC):
            rows = chunk_rows[c]
            for p, cols in enumerate(pair_cols):
                kp = ki_s[rows, cols]
                sc = _dot_nt(qd_s[rows, cols], _block_diag(kp[:, :HG_F], kp[:, HG_F:]))
                sc_s[rows, p * LANES:(p + 1) * LANES] = jnp.where(mask2, sc, 0.0).astype(BF16)

    def states():
        for h in range(HG_HEADS):
            cols = slice(h * HG_F, (h + 1) * HG_F)
            s = s_ref[h]
            for c in order:
                rows = chunk_rows[c]
                sp_s[c, h] = s.astype(BF16)
                s = s * et_s[c * SUBLANES:c * SUBLANES + 1, cols] + _dot_tn(v_ref[rows, cols], ke_s[rows, cols])
            s_ref[h] = s

    def outputs():
        for c in range(HG_NC):
            rows = chunk_rows[c]
            for p, cols in enumerate(pair_cols):
                vp = v_ref[rows, cols]
                o = _dot(sc_s[rows, p * LANES:(p + 1) * LANES], _block_diag(vp[:, :HG_F], vp[:, HG_F:]))
                o = o + _dot_nt(qd_s[rows, cols], _block_diag(sp_s[c, 2 * p], sp_s[c, 2 * p + 1]))
                emit(rows, p, o)

    return [gates, scores, states, outputs]


def _ssd_stages(dtx_ref, bc_ref, dt_ref, a_ref, s_ref, cum_s, cumt_s, sc_s, m_s, dxe_s, ecum_s, sp_s,
                reverse, dcol, emit):
    mask = _tri(SSD_CHUNK, reverse)
    tri_b = jnp.where(mask, 1.0, 0.0).astype(BF16)
    last = slice(0, 1) if reverse else slice(SSD_CHUNK - 1, SSD_CHUNK)
    chunk_rows = [slice(c * SSD_CHUNK, (c + 1) * SSD_CHUNK) for c in range(SSD_NC)]
    order = list(reversed(range(SSD_NC))) if reverse else list(range(SSD_NC))
    lane = lax.broadcasted_iota(jnp.int32, (SSD_CHUNK, LANES), 1)
    low = lane < SSD_HEADDIM

    def decays():
        for c in range(SSD_NC):
            cum = _cumsum_mm(tri_b, dt_ref[chunk_rows[c], :] * (a_ref[...] * LOG2E))
            cum_s[c] = cum
            cumt_s[c] = cum.T

    def scores():
        for c in range(SSD_NC):
            rows = chunk_rows[c]
            for g in range(SSD_GROUPS):
                bg = bc_ref[rows, g * SSD_STATE:(g + 1) * SSD_STATE]
                cg = bc_ref[rows, SSD_BC + g * SSD_STATE:SSD_BC + (g + 1) * SSD_STATE]
                sc_s[c, g] = _dot_nt(cg, bg)

    def masks():
        for c in range(SSD_NC):
            rows = chunk_rows[c]
            cum = cum_s[c]
            cum_t = cumt_s[c]
            for pair in range(SSD_HEADS // 2):
                e_cum, e_end = [], []
                for h in (2 * pair, 2 * pair + 1):
                    col = dcol + h
                    r = jnp.broadcast_to(cum[:, col:col + 1], (SSD_CHUNK, SSD_CHUNK))
                    row_t = cum_t[col:col + 1, :]
                    decay = jnp.exp2(jnp.where(mask, r - row_t, -jnp.inf))
                    m_s[c, h] = (sc_s[c, h // SSD_HPG] * decay).astype(BF16)
                    e_cum.append(jnp.exp2(r))
                    e_end.append(jnp.exp2(row_t[:, last.start:last.stop] - r))
                cols = slice(pair * LANES, (pair + 1) * LANES)
                ecum_s[rows, cols] = jnp.where(low, e_cum[0], e_cum[1])
                dxe_s[rows, cols] = (jnp.where(low, e_end[0], e_end[1])
                                     * dtx_ref[rows, cols].astype(F32)).astype(BF16)

    def states():
        for g in range(SSD_GROUPS):
            grp = slice(g * SSD_GRP_W, (g + 1) * SSD_GRP_W)
            s = s_ref[grp, :]
            for c in order:
                rows = chunk_rows[c]
                sp_s[c, grp, :] = s.astype(BF16)
                ds = _dot_tn(dxe_s[rows, grp], bc_ref[rows, g * SSD_STATE:(g + 1) * SSD_STATE])
                tot_t = cumt_s[c][:, last.start:last.stop]
                dec = jnp.concatenate(
                    [jnp.broadcast_to(jnp.exp2(tot_t[dcol + g * SSD_HPG + hh:dcol + g * SSD_HPG + hh + 1, :]),
                                      (SSD_HEADDIM, SSD_STATE)) for hh in range(SSD_HPG)], axis=0)
                s = s * dec + ds
            s_ref[grp, :] = s

    def outputs():
        for c in range(SSD_NC):
            rows = chunk_rows[c]
            for g in range(SSD_GROUPS):
                grp = slice(g * SSD_GRP_W, (g + 1) * SSD_GRP_W)
                cg = bc_ref[rows, SSD_BC + g * SSD_STATE:SSD_BC + (g + 1) * SSD_STATE]
                y_inter = _dot_nt(cg, sp_s[c, grp, :])
                ys = []
                for pp in range(SSD_HPG // 2):
                    pair = g * (SSD_HPG // 2) + pp
                    cols = slice(pair * LANES, (pair + 1) * LANES)
                    dp = dtx_ref[rows, cols]
                    zero = jnp.zeros_like(dp)
                    rhs = jnp.concatenate([jnp.where(low, dp, zero), jnp.where(low, zero, dp)], axis=0)
                    y_intra = _dot(jnp.concatenate([m_s[c, 2 * pair], m_s[c, 2 * pair + 1]], axis=1), rhs)
                    ys.append(y_intra + ecum_s[rows, cols] * y_inter[:, pp * LANES:(pp + 1) * LANES])
                emit(rows, g, ys)

    return [decays, scores, masks, states, outputs]


def _run_scan_stages(hg, sd):
    for stage in (sd[0], hg[0], sd[1], hg[1], sd[2], hg[2], sd[3], hg[3], sd[4]):
        stage()


_SCAN_SCRATCH = [
    pltpu.VMEM((HG_HEADS, HG_F, HG_F), F32),
    pltpu.VMEM((SSD_WIDTH, SSD_STATE), F32),
    pltpu.VMEM((TILE, D_MODEL), BF16),
    pltpu.VMEM((TILE, D_MODEL), BF16),
    pltpu.VMEM((TILE, D_MODEL), BF16),
    pltpu.VMEM((TILE, HG_HEADS // 2 * LANES), BF16),
    pltpu.VMEM((HG_NC, HG_HEADS, HG_F, HG_F), BF16),
    pltpu.VMEM((HG_NC * SUBLANES, D_MODEL), F32),
    pltpu.VMEM((SSD_NC, SSD_CHUNK, LANES), F32),
    pltpu.VMEM((SSD_NC, LANES, SSD_CHUNK), F32),
    pltpu.VMEM((SSD_NC, SSD_GROUPS, SSD_CHUNK, SSD_CHUNK), F32),
    pltpu.VMEM((SSD_NC, SSD_HEADS, SSD_CHUNK, SSD_CHUNK), BF16),
    pltpu.VMEM((TILE, SSD_WIDTH), BF16),
    pltpu.VMEM((TILE, SSD_WIDTH), F32),
    pltpu.VMEM((SSD_NC, SSD_WIDTH, SSD_STATE), BF16),
]


def _fwd_scan_kernel(q_ref, f_ref, v_ref, dtx_ref, bc_ref, dt_ref, lb_ref, a_ref,
                     o_ref, y_ref, shg_s, sssd_s, qd_s, ki_s, ke_s, hsc_s, hsp_s, et_s,
                     cum_s, cumt_s, ssc_s, m_s, dxe_s, ecum_s, ssp_s):
    t = pl.program_id(1)

    @pl.when(t == 0)
    def _():
        shg_s[...] = jnp.zeros_like(shg_s)
        sssd_s[...] = jnp.zeros_like(sssd_s)

    def emit_hg(rows, p, o):
        o_ref[rows, p * HG_PAIR:(p + 1) * HG_PAIR] = o.astype(BF16)

    def emit_ssd(rows, g, ys):
        for i, y in enumerate(ys):
            y_ref[rows, g * SSD_GRP_W + i * LANES:g * SSD_GRP_W + (i + 1) * LANES] = y.astype(BF16)

    hg = _hgrn_stages(q_ref, f_ref, v_ref, lb_ref, shg_s, qd_s, ki_s, ke_s, hsc_s, hsp_s, et_s, False, emit_hg)
    sd = _ssd_stages(dtx_ref, bc_ref, dt_ref, a_ref, sssd_s, cum_s, cumt_s, ssc_s, m_s, dxe_s, ecum_s, ssp_s,
                     False, 0, emit_ssd)
    _run_scan_stages(hg, sd)


def _fwd_scan(q, f, v, dtx, bc, dt, lb, a_pad):
    tok = lambda b, t: (b, t, 0)
    lat = lambda b, t: (b, jnp.maximum(t - 1, 0), 0)
    return pl.pallas_call(
        _fwd_scan_kernel,
        grid=(BATCH, N_TILES),
        in_specs=[
            pl.BlockSpec((None, TILE, D_MODEL), tok),
            pl.BlockSpec((None, TILE, D_MODEL), tok),
            pl.BlockSpec((None, TILE, D_MODEL), tok),
            pl.BlockSpec((None, TILE, SSD_WIDTH), tok),
            pl.BlockSpec((None, TILE, 2 * SSD_BC), tok),
            pl.BlockSpec((None, TILE, LANES), tok),
            pl.BlockSpec((None, 1, D_MODEL), lambda b, t: (0, 0, 0)),
            _const_spec((1, LANES)),
        ],
        out_specs=[pl.BlockSpec((None, TILE, D_MODEL), lat), pl.BlockSpec((None, TILE, SSD_WIDTH), lat)],
        out_shape=[jax.ShapeDtypeStruct((BATCH, SEQ, D_MODEL), BF16),
                   jax.ShapeDtypeStruct((BATCH, SEQ, SSD_WIDTH), BF16)],
        scratch_shapes=_SCAN_SCRATCH,
        compiler_params=pltpu.CompilerParams(
            dimension_semantics=("parallel", "arbitrary"), vmem_limit_bytes=VMEM_LIMIT),
        name="fwd_scan",
    )(q, f, v, dtx, bc, dt, lb, a_pad)


def _bwd_scan_kernel(q_ref, f_ref, v_ref, dtx_ref, xs_ref, bc_ref, dt_ref, g_ref, z_ref, of_ref, yf_ref,
                     lb_ref, a_ref, hnw_ref, dsk_ref, snw_ref, out_ref,
                     shg_s, sssd_s, qd_s, ki_s, ke_s, hsc_s, hsp_s, et_s,
                     cum_s, cumt_s, ssc_s, m_s, dxe_s, ecum_s, ssp_s):
    s = pl.program_id(1)

    @pl.when(s == 0)
    def _():
        shg_s[...] = jnp.zeros_like(shg_s)
        sssd_s[...] = jnp.zeros_like(sssd_s)

    def emit_hg(rows, p, o_b):
        for i in range(2):
            cols = slice(p * HG_PAIR + i * HG_F, p * HG_PAIR + (i + 1) * HG_F)
            o = of_ref[rows, cols] + o_b[:, i * HG_F:(i + 1) * HG_F]
            out_ref[rows, cols] = (_rms(o, hnw_ref[...]) * g_ref[rows, cols]).astype(BF16)

    def emit_ssd(rows, g, ys):
        us = []
        for i, y_b in enumerate(ys):
            cols = slice(g * SSD_GRP_W + i * LANES, g * SSD_GRP_W + (i + 1) * LANES)
            y = yf_ref[rows, cols] + y_b + dsk_ref[:, cols] * xs_ref[rows, cols]
            us.append(y * z_ref[rows, cols])
        ms = sum(jnp.sum(u * u, axis=-1, keepdims=True) for u in us) / SSD_GRP_W
        inv = lax.rsqrt(ms + EPS)
        for i, u in enumerate(us):
            cols = slice(g * SSD_GRP_W + i * LANES, g * SSD_GRP_W + (i + 1) * LANES)
            out_ref[rows, SSD_WIDTH + cols.start:SSD_WIDTH + cols.stop] = (
                u * inv * snw_ref[:, cols]).astype(BF16)

    hg = _hgrn_stages(q_ref, f_ref, v_ref, lb_ref, shg_s, qd_s, ki_s, ke_s, hsc_s, hsp_s, et_s, True, emit_hg)
    sd = _ssd_stages(dtx_ref, bc_ref, dt_ref, a_ref, sssd_s, cum_s, cumt_s, ssc_s, m_s, dxe_s, ecum_s, ssp_s,
                     True, SSD_HEADS, emit_ssd)
    _run_scan_stages(hg, sd)


def _bwd_scan(q, f, v, dtx, xs, bc, dt, g, z, o_f, y_f, lb, a_pad, hg_norm, d_skip, ssd_norm):
    tok = lambda b, s: (b, jnp.where(s == 0, 0, N_TILES - s), 0)
    fcol = lambda b, s: (b, jnp.where(s == 0, 0, N_TILES - s), 1)
    lat = lambda b, s: (b, jnp.where(s == 0, N_LAT_TILES - 1, N_LAT_TILES - s), 0)
    return pl.pallas_call(
        _bwd_scan_kernel,
        grid=(BATCH, N_TILES),
        in_specs=[
            pl.BlockSpec((None, TILE, D_MODEL), tok),
            pl.BlockSpec((None, TILE, D_MODEL), fcol),
            pl.BlockSpec((None, TILE, D_MODEL), tok),
            pl.BlockSpec((None, TILE, SSD_WIDTH), tok),
            pl.BlockSpec((None, TILE, SSD_WIDTH), tok),
            pl.BlockSpec((None, TILE, 2 * SSD_BC), tok),
            pl.BlockSpec((None, TILE, LANES), tok),
            pl.BlockSpec((None, TILE, D_MODEL), tok),
            pl.BlockSpec((None, TILE, SSD_WIDTH), tok),
            pl.BlockSpec((None, TILE, D_MODEL), lat),
            pl.BlockSpec((None, TILE, SSD_WIDTH), lat),
            pl.BlockSpec((None, 1, D_MODEL), lambda b, s: (1, 0, 0)),
            _const_spec((1, LANES)),
            _const_spec((1, HG_F)),
            _const_spec((1, SSD_WIDTH)),
            _const_spec((1, SSD_WIDTH)),
        ],
        out_specs=pl.BlockSpec((None, TILE, 2 * D_MODEL), lat),
        out_shape=jax.ShapeDtypeStruct((BATCH, SEQ, 2 * D_MODEL), BF16),
        scratch_shapes=_SCAN_SCRATCH,
        compiler_params=pltpu.CompilerParams(
            dimension_semantics=("parallel", "arbitrary"), vmem_limit_bytes=VMEM_LIMIT),
        name="bwd_scan",
    )(q, f, v, dtx, xs, bc, dt, g, z, o_f, y_f, lb, a_pad, hg_norm, d_skip, ssd_norm)


def _out_ffn_kernel(y_ref, x_ref, mod_ref, wo_ref, nf_ref, wg_ref, wu_ref, wd_ref, fn_ref,
                    out_ref, u_s, h_s, acc_s):
    g1 = mod_ref[:, 2 * D_MODEL:3 * D_MODEL]
    sh2 = mod_ref[:, 3 * D_MODEL:4 * D_MODEL]
    sc2 = mod_ref[:, 4 * D_MODEL:5 * D_MODEL]
    g2 = mod_ref[:, 5 * D_MODEL:6 * D_MODEL]
    halves = [slice(i * TILE, (i + 1) * TILE) for i in range(FFN_TILE // TILE)]
    n_chunks = D_FF // FFN_CHUNK

    for r in halves:
        h = x_ref[r, :] + g1 * _dot(y_ref[r, :], wo_ref[...])
        h_s[r, :] = h
        u_s[r, :] = (_rms(h, nf_ref[...]) * (1.0 + sc2) + sh2).astype(BF16)

    for r in halves:
        def gate_up(c):
            cols = slice(c * FFN_CHUNK, (c + 1) * FFN_CHUNK)
            return _dot(u_s[r, :], wg_ref[:, cols]), _dot(u_s[r, :], wu_ref[:, cols])

        cur = gate_up(0)
        for c in range(n_chunks):
            nxt = gate_up(c + 1) if c + 1 < n_chunks else None
            part = _dot((_silu(cur[0]) * cur[1]).astype(BF16), wd_ref[c * FFN_CHUNK:(c + 1) * FFN_CHUNK, :])
            if c == 0:
                acc_s[r, :] = part
            else:
                acc_s[r, :] += part
            cur = nxt
        out_ref[r, :] = _rms(h_s[r, :] + g2 * acc_s[r, :], fn_ref[...])


def _out_ffn(ymix, x, mod3, w_out, norm_ffn, w_gate, w_up, w_down, final_norm):
    tok = lambda b, t: (b, t, 0)
    return pl.pallas_call(
        _out_ffn_kernel,
        grid=(BATCH, SEQ // FFN_TILE),
        in_specs=[
            pl.BlockSpec((None, FFN_TILE, 2 * D_MODEL), tok),
            pl.BlockSpec((None, FFN_TILE, D_MODEL), tok),
            pl.BlockSpec((None, 1, N_MOD * D_MODEL), lambda b, t: (b, 0, 0)),
            _const_spec((2 * D_MODEL, D_MODEL)),
            _const_spec((1, D_MODEL)),
            _const_spec((D_MODEL, D_FF)),
            _const_spec((D_MODEL, D_FF)),
            _const_spec((D_FF, D_MODEL)),
            _const_spec((1, D_MODEL)),
        ],
        out_specs=pl.BlockSpec((None, FFN_TILE, D_MODEL), tok),
        out_shape=jax.ShapeDtypeStruct((BATCH, SEQ, D_MODEL), F32),
        scratch_shapes=[pltpu.VMEM((FFN_TILE, D_MODEL), BF16), pltpu.VMEM((FFN_TILE, D_MODEL), F32),
                        pltpu.VMEM((FFN_TILE, D_MODEL), F32)],
        compiler_params=pltpu.CompilerParams(
            dimension_semantics=("parallel", "arbitrary"), vmem_limit_bytes=VMEM_LIMIT),
        name="out_ffn",
    )(ymix, x, mod3, w_out, norm_ffn, w_gate, w_up, w_down, final_norm)


def kernel(x, c, ctx, c_ctx, w_ada, b_ada, norm_mix, w_in, conv_w, conv_b, ssd_a_log, ssd_dt_bias,
           ssd_d, ssd_norm, hgrn_lb_raw, hgrn_norm, w_out, norm_ffn, w_gate, w_up, w_down, final_norm):
    assert x.shape == (BATCH, SEQ, D_MODEL) and ctx.shape == (BATCH, CTX_LEN, D_MODEL)
    assert w_ada.shape == (1, D_MODEL, N_MOD * D_MODEL) and w_in.shape == (1, D_MODEL, D_IN)

    c_all = jnp.concatenate([c, c_ctx[None, :], jnp.zeros((MOD_ROWS - BATCH - 1, D_MODEL), F32)], axis=0)
    w_in_b = w_in[0].astype(BF16)
    w_dt = jnp.pad(w_in_b[:, COL_DT:], ((0, 0), (0, LANES - 2 * SSD_HEADS)))
    dt_bias = jnp.pad(ssd_dt_bias[0].reshape(1, 2 * SSD_HEADS), ((0, 0), (0, LANES - 2 * SSD_HEADS)))
    a_pad = jnp.pad(-jnp.exp(ssd_a_log[0].astype(F32)).reshape(1, 2 * SSD_HEADS),
                    ((0, 0), (0, LANES - 2 * SSD_HEADS)))
    lbs = jnp.cumsum(jax.nn.softmax(hgrn_lb_raw.astype(F32), axis=0), axis=0)[0][:, None, :]
    d_skip = jnp.repeat(ssd_d[0], SSD_HEADDIM)[None, :]

    mod = _ada(c_all, w_ada, b_ada)
    mod3 = mod.reshape(MOD_ROWS, 1, N_MOD * D_MODEL)
    q, f, v, g, z, xs, dtx_f, dtx_b, bc, dt = _in_proj(
        x, ctx, mod3, norm_mix, w_in_b, w_dt, conv_w[0], conv_b, dt_bias)
    o_f, y_f = _fwd_scan(q, f, v, dtx_f, bc, dt, lbs, a_pad)
    ymix = _bwd_scan(q, f, v, dtx_b, xs, bc, dt, g, z, o_f, y_f, lbs, a_pad,
                     hgrn_norm, d_skip, ssd_norm)
    return _out_ffn(ymix, x, mod3, w_out[0].astype(BF16), norm_ffn, w_gate[0].astype(BF16),
                    w_up[0].astype(BF16), w_down[0].astype(BF16), final_norm[None, :])
```

```python
import jax
import jax.numpy as jnp
from jax import lax
from jax.experimental import pallas as pl
from jax.experimental.pallas import tpu as pltpu

F32 = jnp.float32
BF16 = jnp.bfloat16

D_MODEL = 1024
BATCH = 8
SEQ = 4096
CTX_LEN = 256
GRID_W = 64
HG_HEADS = 8
HG_F = 128
HG_CHUNK = 64
SSD_HEADS = 16
SSD_HEADDIM = 64
SSD_GROUPS = 4
SSD_HPG = SSD_HEADS // SSD_GROUPS
SSD_STATE = 128
SSD_CONV = 5
SSD_CHUNK = 128
SSD_WIDTH = SSD_HEADS * SSD_HEADDIM
SSD_BC = SSD_GROUPS * SSD_STATE
D_FF = 2816
N_MOD = 6
EPS = 1e-6
LOG2E = 1.4426950408889634

LANES = 128
SUBLANES = 8

TILE = 256
N_LAT_TILES = SEQ // TILE
N_TILES = N_LAT_TILES + 1
MOD_ROWS = 16
CTX_ROW = BATCH
HG_NC = TILE // HG_CHUNK
SSD_NC = TILE // SSD_CHUNK
HG_PAIR = 2 * HG_F
SSD_GRP_W = SSD_HPG * SSD_HEADDIM

COL_Q = 0
COL_F = 1024
COL_I = 3072
COL_G = 4096
COL_Z = 5120
COL_XBC = 6144
COL_DT = 8192
D_IN = 8224
PROJ_CHUNK = 512
CONV_HALO = SUBLANES
FFN_CHUNK = 256
FFN_TILE = 2 * TILE

VMEM_LIMIT = 56 * 1024 * 1024


def _silu(x):
    return x * jax.nn.sigmoid(x)


def _rms(x, w):
    return x * lax.rsqrt(jnp.mean(x * x, axis=-1, keepdims=True) + EPS) * w


def _dot(a, b):
    return jnp.dot(a, b, preferred_element_type=F32)


def _dot_nt(a, b):
    return lax.dot_general(a, b, (((1,), (1,)), ((), ())), preferred_element_type=F32)


def _dot_tn(a, b):
    return lax.dot_general(a, b, (((0,), (0,)), ((), ())), preferred_element_type=F32)


def _cumsum_mm(tri_b, x):
    hi = x.astype(BF16)
    r1 = x - hi.astype(F32)
    mid = r1.astype(BF16)
    lo = (r1 - mid.astype(F32)).astype(BF16)
    return _dot(tri_b, hi) + _dot(tri_b, mid) + _dot(tri_b, lo)


def _block_diag(a, b):
    z = jnp.zeros_like(a)
    return jnp.concatenate([jnp.concatenate([a, z], axis=1), jnp.concatenate([z, b], axis=1)], axis=0)


def _lane_pair(col_a, col_b):
    n = col_a.shape[0]
    lane = lax.broadcasted_iota(jnp.int32, (n, LANES), 1)
    return jnp.where(lane < SSD_HEADDIM, col_a, col_b)


def _const_spec(shape):
    nd = len(shape)
    return pl.BlockSpec(shape, lambda *_: (0,) * nd, pipeline_mode=pl.Buffered(1))


def _ada_kernel(c_ref, w_ref, b_ref, o_ref):
    act = _silu(c_ref[...]).astype(BF16)
    o_ref[...] = _dot(act, w_ref[...].astype(BF16)) + b_ref[...]


def _ada(c_all, w_ada, b_ada):
    n = N_MOD * D_MODEL
    bn = 1536
    return pl.pallas_call(
        _ada_kernel,
        grid=(n // bn,),
        in_specs=[
            pl.BlockSpec((MOD_ROWS, D_MODEL), lambda j: (0, 0)),
            pl.BlockSpec((None, D_MODEL, bn), lambda j: (0, 0, j)),
            pl.BlockSpec((1, bn), lambda j: (0, j)),
        ],
        out_specs=pl.BlockSpec((MOD_ROWS, bn), lambda j: (0, j)),
        out_shape=jax.ShapeDtypeStruct((MOD_ROWS, n), F32),
        compiler_params=pltpu.CompilerParams(vmem_limit_bytes=VMEM_LIMIT),
        name="ada",
    )(c_all, w_ada, b_ada)


def _in_proj_kernel(x_ref, ctx_ref, mod_ref, nw_ref, w_ref, wdt_ref, cw_ref, cb_ref, dtb_ref,
                    q_ref, f_ref, v_ref, g_ref, z_ref, xs_ref, dxf_ref, dxb_ref, bc_ref, dt_ref,
                    u_s, xbc_s):
    t = pl.program_id(1)
    is_ctx = t == 0
    x = jnp.where(is_ctx, ctx_ref[...], x_ref[...])
    shift = mod_ref[:, 0:D_MODEL]
    scale = mod_ref[:, D_MODEL:2 * D_MODEL]
    u_s[...] = (_rms(x, nw_ref[...]) * (1.0 + scale) + shift).astype(BF16)

    def proj(c0, width):
        return _dot(u_s[...], w_ref[:, c0:c0 + width])

    dt = jax.nn.softplus(_dot(u_s[...], wdt_ref[...]) + dtb_ref[...])
    dt_ref[...] = dt
    zero_halo = jnp.zeros((CONV_HALO, SSD_WIDTH + 2 * SSD_BC), F32)
    xbc_s[0:CONV_HALO, :] = zero_halo
    xbc_s[CONV_HALO + TILE:, :] = zero_halo
    for c0 in range(COL_XBC, COL_DT, PROJ_CHUNK):
        xbc_s[CONV_HALO:CONV_HALO + TILE, c0 - COL_XBC:c0 - COL_XBC + PROJ_CHUNK] = proj(c0, PROJ_CHUNK)

    def proj_chunk(c0):
        y = proj(c0, PROJ_CHUNK)
        if c0 < COL_F:
            q_ref[:, c0 - COL_Q:c0 - COL_Q + PROJ_CHUNK] = (_silu(y) * (HG_F ** -0.5)).astype(BF16)
        elif c0 < COL_I:
            f_ref[:, c0 - COL_F:c0 - COL_F + PROJ_CHUNK] = y
        elif c0 < COL_G:
            v_ref[:, c0 - COL_I:c0 - COL_I + PROJ_CHUNK] = y.astype(BF16)
        elif c0 < COL_Z:
            g_ref[:, c0 - COL_G:c0 - COL_G + PROJ_CHUNK] = _silu(y).astype(BF16)
        else:
            z_ref[:, c0 - COL_Z:c0 - COL_Z + PROJ_CHUNK] = _silu(y).astype(BF16)

    pad = SSD_CONV // 2

    def conv_slab(j):
        cols = slice(j * LANES, (j + 1) * LANES)
        blk_rows = GRID_W + 2 * CONV_HALO
        for r0 in range(0, TILE, GRID_W):
            rows = slice(r0, r0 + GRID_W)
            blk = xbc_s[r0:r0 + blk_rows, cols]
            below = jnp.where(is_ctx, blk[:CONV_HALO], 0.0)
            above = jnp.where(is_ctx, blk[CONV_HALO + GRID_W:], 0.0)
            blk = jnp.concatenate([below, blk[CONV_HALO:CONV_HALO + GRID_W], above], axis=0)
            acc = jnp.broadcast_to(cb_ref[:, cols], (GRID_W, LANES))
            for k in range(SSD_CONV):
                win = blk if k == pad else pltpu.roll(blk, (pad - k) % blk_rows, axis=0)
                acc = acc + cw_ref[k:k + 1, cols] * win[CONV_HALO:CONV_HALO + GRID_W]
            act = _silu(acc)
            if j < SSD_WIDTH // LANES:
                xs_ref[rows, cols] = act.astype(BF16)
                for dref, h0 in ((dxf_ref, 2 * j), (dxb_ref, SSD_HEADS + 2 * j)):
                    dref[rows, cols] = (_lane_pair(dt[rows, h0:h0 + 1], dt[rows, h0 + 1:h0 + 2]) * act).astype(BF16)
            else:
                bc_ref[rows, j * LANES - SSD_WIDTH:(j + 1) * LANES - SSD_WIDTH] = act.astype(BF16)

    for j in range((SSD_WIDTH + 2 * SSD_BC) // LANES):
        conv_slab(j)
    for c0 in range(COL_Q, COL_XBC, PROJ_CHUNK):
        proj_chunk(c0)


def _in_proj(x, ctx, mod3, norm_w, w_in, w_dt, conv_w, conv_b, dt_bias):
    lat = lambda b, t: (b, jnp.maximum(t - 1, 0), 0)
    tok = lambda b, t: (b, t, 0)
    n_tok = N_TILES * TILE

    def out(width, dtype):
        return jax.ShapeDtypeStruct((BATCH, n_tok, width), dtype), pl.BlockSpec((None, TILE, width), tok)

    outs = [out(D_MODEL, BF16), out(2 * D_MODEL, F32), out(D_MODEL, BF16), out(D_MODEL, BF16),
            out(D_MODEL, BF16), out(SSD_WIDTH, BF16), out(SSD_WIDTH, BF16), out(SSD_WIDTH, BF16),
            out(2 * SSD_BC, BF16), out(LANES, F32)]
    return pl.pallas_call(
        _in_proj_kernel,
        grid=(BATCH, N_TILES),
        in_specs=[
            pl.BlockSpec((None, TILE, D_MODEL), lat),
            pl.BlockSpec((None, CTX_LEN, D_MODEL), lambda b, t: (b, 0, 0)),
            pl.BlockSpec((None, 1, N_MOD * D_MODEL), lambda b, t: (jnp.where(t == 0, CTX_ROW, b), 0, 0)),
            _const_spec((1, D_MODEL)),
            _const_spec((D_MODEL, D_IN)),
            _const_spec((D_MODEL, LANES)),
            _const_spec((SSD_CONV, SSD_WIDTH + 2 * SSD_BC)),
            _const_spec((1, SSD_WIDTH + 2 * SSD_BC)),
            _const_spec((1, LANES)),
        ],
        out_specs=[o[1] for o in outs],
        out_shape=[o[0] for o in outs],
        scratch_shapes=[
            pltpu.VMEM((TILE, D_MODEL), BF16),
            pltpu.VMEM((TILE + 2 * CONV_HALO, SSD_WIDTH + 2 * SSD_BC), F32),
        ],
        compiler_params=pltpu.CompilerParams(
            dimension_semantics=("parallel", "arbitrary"), vmem_limit_bytes=VMEM_LIMIT),
        name="in_proj",
    )(x, ctx, mod3, norm_w, w_in, w_dt, conv_w, conv_b, dt_bias)


def _tri(n, reverse):
    i = lax.broadcasted_iota(jnp.int32, (n, n), 0)
    j = lax.broadcasted_iota(jnp.int32, (n, n), 1)
    return (j >= i) if reverse else (j <= i)


def _hgrn_stages(q_ref, f_ref, v_ref, lb_ref, s_ref, qd_s, ki_s, ke_s, sc_s, sp_s, et_s, reverse, emit):
    mask = _tri(HG_CHUNK, reverse)
    tri_b = jnp.where(mask, 1.0, 0.0).astype(BF16)
    i2 = lax.broadcasted_iota(jnp.int32, (HG_CHUNK, LANES), 0)
    j2 = lax.broadcasted_iota(jnp.int32, (HG_CHUNK, LANES), 1) & (HG_CHUNK - 1)
    mask2 = (j2 >= i2) if reverse else (j2 <= i2)
    last = slice(0, 1) if reverse else slice(HG_CHUNK - 1, HG_CHUNK)
    chunk_rows = [slice(c * HG_CHUNK, (c + 1) * HG_CHUNK) for c in range(HG_NC)]
    pair_cols = [slice(p * HG_PAIR, (p + 1) * HG_PAIR) for p in range(HG_HEADS // 2)]
    order = list(reversed(range(HG_NC))) if reverse else list(range(HG_NC))

    def gates():
        for c in range(HG_NC):
            rows = chunk_rows[c]
            for cols in pair_cols:
                lb = lb_ref[:, cols]
                f = lb + (1.0 - lb) * jax.nn.sigmoid(f_ref[rows, cols])
                k = 1.0 - f
                cum = _cumsum_mm(tri_b, jnp.log(f)) * LOG2E
                e_tot = jnp.exp2(cum[last, :])
                k_inv = k * jnp.exp2(-cum)
                qd_s[rows, cols] = (q_ref[rows, cols] * jnp.exp2(cum)).astype(BF16)
                ki_s[rows, cols] = k_inv.astype(BF16)
                ke_s[rows, cols] = (k_inv * e_tot).astype(BF16)
                et_s[c * SUBLANES:c * SUBLANES + 1, cols] = e_tot

    def scores():
        for c in range(HG_NC):
            rows = chunk_rows[c]
            for p, cols in enumerate(pair_cols):
                kp = ki_s[rows, cols]
                sc = _dot_nt(qd_s[rows, cols], _block_diag(kp[:, :HG_F], kp[:, HG_F:]))
                sc_s[rows, p * LANES:(p + 1) * LANES] = jnp.where(mask2, sc, 0.0).astype(BF16)

    def states():
        for h in range(HG_HEADS):
            cols = slice(h * HG_F, (h + 1) * HG_F)
            s = s_ref[h]
            for c in order:
                rows = chunk_rows[c]
                sp_s[c, h] = s.astype(BF16)
                s = s * et_s[c * SUBLANES:c * SUBLANES + 1, cols] + _dot_tn(v_ref[rows, cols], ke_s[rows, cols])
            s_ref[h] = s

    def outputs():
        for c in range(HG_NC):
            rows = chunk_rows[c]
            for p, cols in enumerate(pair_cols):
                vp = v_ref[rows, cols]
                o = _dot(sc_s[rows, p * LANES:(p + 1) * LANES], _block_diag(vp[:, :HG_F], vp[:, HG_F:]))
                o = o + _dot_nt(qd_s[rows, cols], _block_diag(sp_s[c, 2 * p], sp_s[c, 2 * p + 1]))
                emit(rows, p, o)

    return [gates, scores, states, outputs]


def _ssd_stages(dtx_ref, bc_ref, dt_ref, a_ref, s_ref, cum_s, cumt_s, sc_s, m_s, dxe_s, ecum_s, sp_s,
                reverse, dcol, emit):
    mask = _tri(SSD_CHUNK, reverse)
    tri_b = jnp.where(mask, 1.0, 0.0).astype(BF16)
    last = slice(0, 1) if reverse else slice(SSD_CHUNK - 1, SSD_CHUNK)
    chunk_rows = [slice(c * SSD_CHUNK, (c + 1) * SSD_CHUNK) for c in range(SSD_NC)]
    order = list(reversed(range(SSD_NC))) if reverse else list(range(SSD_NC))
    lane = lax.broadcasted_iota(jnp.int32, (SSD_CHUNK, LANES), 1)
    low = lane < SSD_HEADDIM

    def decays():
        for c in range(SSD_NC):
            cum = _cumsum_mm(tri_b, dt_ref[chunk_rows[c], :] * (a_ref[...] * LOG2E))
            cum_s[c] = cum
            cumt_s[c] = cum.T

    def scores():
        for c in range(SSD_NC):
            rows = chunk_rows[c]
            for g in range(SSD_GROUPS):
                bg = bc_ref[rows, g * SSD_STATE:(g + 1) * SSD_STATE]
                cg = bc_ref[rows, SSD_BC + g * SSD_STATE:SSD_BC + (g + 1) * SSD_STATE]
                sc_s[c, g] = _dot_nt(cg, bg)

    def masks():
        for c in range(SSD_NC):
            rows = chunk_rows[c]
            cum = cum_s[c]
            cum_t = cumt_s[c]
            for pair in range(SSD_HEADS // 2):
                rs, tots = [], []
                for h in (2 * pair, 2 * pair + 1):
                    col = dcol + h
                    r = jnp.broadcast_to(cum[:, col:col + 1], (SSD_CHUNK, SSD_CHUNK))
                    row_t = cum_t[col:col + 1, :]
                    decay = jnp.exp2(jnp.where(mask, r - row_t, -jnp.inf))
                    m_s[c, h] = (sc_s[c, h // SSD_HPG] * decay).astype(BF16)
                    rs.append(r)
                    tots.append(row_t[:, last.start:last.stop])
                r2 = jnp.where(low, rs[0], rs[1])
                tot2 = jnp.where(low[0:1, :], tots[0], tots[1])
                cols = slice(pair * LANES, (pair + 1) * LANES)
                ecum_s[rows, cols] = jnp.exp2(r2)
                dxe_s[rows, cols] = (jnp.exp2(tot2 - r2) * dtx_ref[rows, cols].astype(F32)).astype(BF16)

    def states():
        for g in range(SSD_GROUPS):
            grp = slice(g * SSD_GRP_W, (g + 1) * SSD_GRP_W)
            s = s_ref[grp, :]
            for c in order:
                rows = chunk_rows[c]
                sp_s[c, grp, :] = s.astype(BF16)
                ds = _dot_tn(dxe_s[rows, grp], bc_ref[rows, g * SSD_STATE:(g + 1) * SSD_STATE])
                tot_t = cumt_s[c][:, last.start:last.stop]
                dec = jnp.concatenate(
                    [jnp.broadcast_to(jnp.exp2(tot_t[dcol + g * SSD_HPG + hh:dcol + g * SSD_HPG + hh + 1, :]),
                                      (SSD_HEADDIM, SSD_STATE)) for hh in range(SSD_HPG)], axis=0)
                s = s * dec + ds
            s_ref[grp, :] = s

    def outputs():
        for c in range(SSD_NC):
            rows = chunk_rows[c]
            for g in range(SSD_GROUPS):
                grp = slice(g * SSD_GRP_W, (g + 1) * SSD_GRP_W)
                cg = bc_ref[rows, SSD_BC + g * SSD_STATE:SSD_BC + (g + 1) * SSD_STATE]
                y_inter = _dot_nt(cg, sp_s[c, grp, :])
                ys = []
                for pp in range(SSD_HPG // 2):
                    pair = g * (SSD_HPG // 2) + pp
                    cols = slice(pair * LANES, (pair + 1) * LANES)
                    dp = dtx_ref[rows, cols]
                    zero = jnp.zeros_like(dp)
                    rhs = jnp.concatenate([jnp.where(low, dp, zero), jnp.where(low, zero, dp)], axis=0)
                    y_intra = _dot(jnp.concatenate([m_s[c, 2 * pair], m_s[c, 2 * pair + 1]], axis=1), rhs)
                    ys.append(y_intra + ecum_s[rows, cols] * y_inter[:, pp * LANES:(pp + 1) * LANES])
                emit(rows, g, ys)

    return [decays, scores, masks, states, outputs]


def _run_scan_stages(hg, sd):
    for stage in (sd[0], hg[0], sd[1], hg[1], sd[2], hg[2], sd[3], hg[3], sd[4]):
        stage()


_SCAN_SCRATCH = [
    pltpu.VMEM((HG_HEADS, HG_F, HG_F), F32),
    pltpu.VMEM((SSD_WIDTH, SSD_STATE), F32),
    pltpu.VMEM((TILE, D_MODEL), BF16),
    pltpu.VMEM((TILE, D_MODEL), BF16),
    pltpu.VMEM((TILE, D_MODEL), BF16),
    pltpu.VMEM((TILE, HG_HEADS // 2 * LANES), BF16),
    pltpu.VMEM((HG_NC, HG_HEADS, HG_F, HG_F), BF16),
    pltpu.VMEM((HG_NC * SUBLANES, D_MODEL), F32),
    pltpu.VMEM((SSD_NC, SSD_CHUNK, LANES), F32),
    pltpu.VMEM((SSD_NC, LANES, SSD_CHUNK), F32),
    pltpu.VMEM((SSD_NC, SSD_GROUPS, SSD_CHUNK, SSD_CHUNK), F32),
    pltpu.VMEM((SSD_NC, SSD_HEADS, SSD_CHUNK, SSD_CHUNK), BF16),
    pltpu.VMEM((TILE, SSD_WIDTH), BF16),
    pltpu.VMEM((TILE, SSD_WIDTH), F32),
    pltpu.VMEM((SSD_NC, SSD_WIDTH, SSD_STATE), BF16),
]


def _fwd_scan_kernel(q_ref, f_ref, v_ref, dtx_ref, bc_ref, dt_ref, lb_ref, a_ref,
                     o_ref, y_ref, shg_s, sssd_s, qd_s, ki_s, ke_s, hsc_s, hsp_s, et_s,
                     cum_s, cumt_s, ssc_s, m_s, dxe_s, ecum_s, ssp_s):
    t = pl.program_id(1)

    @pl.when(t == 0)
    def _():
        shg_s[...] = jnp.zeros_like(shg_s)
        sssd_s[...] = jnp.zeros_like(sssd_s)

    def emit_hg(rows, p, o):
        o_ref[rows, p * HG_PAIR:(p + 1) * HG_PAIR] = o.astype(BF16)

    def emit_ssd(rows, g, ys):
        for i, y in enumerate(ys):
            y_ref[rows, g * SSD_GRP_W + i * LANES:g * SSD_GRP_W + (i + 1) * LANES] = y.astype(BF16)

    hg = _hgrn_stages(q_ref, f_ref, v_ref, lb_ref, shg_s, qd_s, ki_s, ke_s, hsc_s, hsp_s, et_s, False, emit_hg)
    sd = _ssd_stages(dtx_ref, bc_ref, dt_ref, a_ref, sssd_s, cum_s, cumt_s, ssc_s, m_s, dxe_s, ecum_s, ssp_s,
                     False, 0, emit_ssd)
    _run_scan_stages(hg, sd)


def _fwd_scan(q, f, v, dtx, bc, dt, lb, a_pad):
    tok = lambda b, t: (b, t, 0)
    lat = lambda b, t: (b, jnp.maximum(t - 1, 0), 0)
    return pl.pallas_call(
        _fwd_scan_kernel,
        grid=(BATCH, N_TILES),
        in_specs=[
            pl.BlockSpec((None, TILE, D_MODEL), tok),
            pl.BlockSpec((None, TILE, D_MODEL), tok),
            pl.BlockSpec((None, TILE, D_MODEL), tok),
            pl.BlockSpec((None, TILE, SSD_WIDTH), tok),
            pl.BlockSpec((None, TILE, 2 * SSD_BC), tok),
            pl.BlockSpec((None, TILE, LANES), tok),
            pl.BlockSpec((None, 1, D_MODEL), lambda b, t: (0, 0, 0)),
            _const_spec((1, LANES)),
        ],
        out_specs=[pl.BlockSpec((None, TILE, D_MODEL), lat), pl.BlockSpec((None, TILE, SSD_WIDTH), lat)],
        out_shape=[jax.ShapeDtypeStruct((BATCH, SEQ, D_MODEL), BF16),
                   jax.ShapeDtypeStruct((BATCH, SEQ, SSD_WIDTH), BF16)],
        scratch_shapes=_SCAN_SCRATCH,
        compiler_params=pltpu.CompilerParams(
            dimension_semantics=("parallel", "arbitrary"), vmem_limit_bytes=VMEM_LIMIT),
        name="fwd_scan",
    )(q, f, v, dtx, bc, dt, lb, a_pad)


def _bwd_scan_kernel(q_ref, f_ref, v_ref, dtx_ref, xs_ref, bc_ref, dt_ref, g_ref, z_ref, of_ref, yf_ref,
                     lb_ref, a_ref, hnw_ref, dsk_ref, snw_ref, out_ref,
                     shg_s, sssd_s, qd_s, ki_s, ke_s, hsc_s, hsp_s, et_s,
                     cum_s, cumt_s, ssc_s, m_s, dxe_s, ecum_s, ssp_s):
    s = pl.program_id(1)

    @pl.when(s == 0)
    def _():
        shg_s[...] = jnp.zeros_like(shg_s)
        sssd_s[...] = jnp.zeros_like(sssd_s)

    def emit_hg(rows, p, o_b):
        for i in range(2):
            cols = slice(p * HG_PAIR + i * HG_F, p * HG_PAIR + (i + 1) * HG_F)
            o = of_ref[rows, cols] + o_b[:, i * HG_F:(i + 1) * HG_F]
            out_ref[rows, cols] = (_rms(o, hnw_ref[...]) * g_ref[rows, cols]).astype(BF16)

    def emit_ssd(rows, g, ys):
        us = []
        for i, y_b in enumerate(ys):
            cols = slice(g * SSD_GRP_W + i * LANES, g * SSD_GRP_W + (i + 1) * LANES)
            y = yf_ref[rows, cols] + y_b + dsk_ref[:, cols] * xs_ref[rows, cols]
            us.append(y * z_ref[rows, cols])
        ms = sum(jnp.sum(u * u, axis=-1, keepdims=True) for u in us) / SSD_GRP_W
        inv = lax.rsqrt(ms + EPS)
        for i, u in enumerate(us):
            cols = slice(g * SSD_GRP_W + i * LANES, g * SSD_GRP_W + (i + 1) * LANES)
            out_ref[rows, SSD_WIDTH + cols.start:SSD_WIDTH + cols.stop] = (
                u * inv * snw_ref[:, cols]).astype(BF16)

    hg = _hgrn_stages(q_ref, f_ref, v_ref, lb_ref, shg_s, qd_s, ki_s, ke_s, hsc_s, hsp_s, et_s, True, emit_hg)
    sd = _ssd_stages(dtx_ref, bc_ref, dt_ref, a_ref, sssd_s, cum_s, cumt_s, ssc_s, m_s, dxe_s, ecum_s, ssp_s,
                     True, SSD_HEADS, emit_ssd)
    _run_scan_stages(hg, sd)


def _bwd_scan(q, f, v, dtx, xs, bc, dt, g, z, o_f, y_f, lb, a_pad, hg_norm, d_skip, ssd_norm):
    tok = lambda b, s: (b, jnp.where(s == 0, 0, N_TILES - s), 0)
    fcol = lambda b, s: (b, jnp.where(s == 0, 0, N_TILES - s), 1)
    lat = lambda b, s: (b, jnp.where(s == 0, N_LAT_TILES - 1, N_LAT_TILES - s), 0)
    return pl.pallas_call(
        _bwd_scan_kernel,
        grid=(BATCH, N_TILES),
        in_specs=[
            pl.BlockSpec((None, TILE, D_MODEL), tok),
            pl.BlockSpec((None, TILE, D_MODEL), fcol),
            pl.BlockSpec((None, TILE, D_MODEL), tok),
            pl.BlockSpec((None, TILE, SSD_WIDTH), tok),
            pl.BlockSpec((None, TILE, SSD_WIDTH), tok),
            pl.BlockSpec((None, TILE, 2 * SSD_BC), tok),
            pl.BlockSpec((None, TILE, LANES), tok),
            pl.BlockSpec((None, TILE, D_MODEL), tok),
            pl.BlockSpec((None, TILE, SSD_WIDTH), tok),
            pl.BlockSpec((None, TILE, D_MODEL), lat),
            pl.BlockSpec((None, TILE, SSD_WIDTH), lat),
            pl.BlockSpec((None, 1, D_MODEL), lambda b, s: (1, 0, 0)),
            _const_spec((1, LANES)),
            _const_spec((1, HG_F)),
            _const_spec((1, SSD_WIDTH)),
            _const_spec((1, SSD_WIDTH)),
        ],
        out_specs=pl.BlockSpec((None, TILE, 2 * D_MODEL), lat),
        out_shape=jax.ShapeDtypeStruct((BATCH, SEQ, 2 * D_MODEL), BF16),
        scratch_shapes=_SCAN_SCRATCH,
        compiler_params=pltpu.CompilerParams(
            dimension_semantics=("parallel", "arbitrary"), vmem_limit_bytes=VMEM_LIMIT),
        name="bwd_scan",
    )(q, f, v, dtx, xs, bc, dt, g, z, o_f, y_f, lb, a_pad, hg_norm, d_skip, ssd_norm)


def _out_ffn_kernel(y_ref, x_ref, mod_ref, wo_ref, nf_ref, wg_ref, wu_ref, wd_ref, fn_ref,
                    out_ref, u_s, h_s, acc_s):
    g1 = mod_ref[:, 2 * D_MODEL:3 * D_MODEL]
    sh2 = mod_ref[:, 3 * D_MODEL:4 * D_MODEL]
    sc2 = mod_ref[:, 4 * D_MODEL:5 * D_MODEL]
    g2 = mod_ref[:, 5 * D_MODEL:6 * D_MODEL]
    halves = [slice(i * TILE, (i + 1) * TILE) for i in range(FFN_TILE // TILE)]
    n_chunks = D_FF // FFN_CHUNK

    for r in halves:
        h = x_ref[r, :] + g1 * _dot(y_ref[r, :], wo_ref[...])
        h_s[r, :] = h
        u_s[r, :] = (_rms(h, nf_ref[...]) * (1.0 + sc2) + sh2).astype(BF16)

    for r in halves:
        def gate_up(c):
            cols = slice(c * FFN_CHUNK, (c + 1) * FFN_CHUNK)
            return _dot(u_s[r, :], wg_ref[:, cols]), _dot(u_s[r, :], wu_ref[:, cols])

        cur = gate_up(0)
        for c in range(n_chunks):
            nxt = gate_up(c + 1) if c + 1 < n_chunks else None
            part = _dot((_silu(cur[0]) * cur[1]).astype(BF16), wd_ref[c * FFN_CHUNK:(c + 1) * FFN_CHUNK, :])
            if c == 0:
                acc_s[r, :] = part
            else:
                acc_s[r, :] += part
            cur = nxt
        out_ref[r, :] = _rms(h_s[r, :] + g2 * acc_s[r, :], fn_ref[...])


def _out_ffn(ymix, x, mod3, w_out, norm_ffn, w_gate, w_up, w_down, final_norm):
    tok = lambda b, t: (b, t, 0)
    return pl.pallas_call(
        _out_ffn_kernel,
        grid=(BATCH, SEQ // FFN_TILE),
        in_specs=[
            pl.BlockSpec((None, FFN_TILE, 2 * D_MODEL), tok),
            pl.BlockSpec((None, FFN_TILE, D_MODEL), tok),
            pl.BlockSpec((None, 1, N_MOD * D_MODEL), lambda b, t: (b, 0, 0)),
            _const_spec((2 * D_MODEL, D_MODEL)),
            _const_spec((1, D_MODEL)),
            _const_spec((D_MODEL, D_FF)),
            _const_spec((D_MODEL, D_FF)),
            _const_spec((D_FF, D_MODEL)),
            _const_spec((1, D_MODEL)),
        ],
        out_specs=pl.BlockSpec((None, FFN_TILE, D_MODEL), tok),
        out_shape=jax.ShapeDtypeStruct((BATCH, SEQ, D_MODEL), F32),
        scratch_shapes=[pltpu.VMEM((FFN_TILE, D_MODEL), BF16), pltpu.VMEM((FFN_TILE, D_MODEL), F32),
                        pltpu.VMEM((FFN_TILE, D_MODEL), F32)],
        compiler_params=pltpu.CompilerParams(
            dimension_semantics=("parallel", "arbitrary"), vmem_limit_bytes=VMEM_LIMIT),
        name="out_ffn",
    )(ymix, x, mod3, w_out, norm_ffn, w_gate, w_up, w_down, final_norm)


def kernel(x, c, ctx, c_ctx, w_ada, b_ada, norm_mix, w_in, conv_w, conv_b, ssd_a_log, ssd_dt_bias,
           ssd_d, ssd_norm, hgrn_lb_raw, hgrn_norm, w_out, norm_ffn, w_gate, w_up, w_down, final_norm):
    assert x.shape == (BATCH, SEQ, D_MODEL) and ctx.shape == (BATCH, CTX_LEN, D_MODEL)
    assert w_ada.shape == (1, D_MODEL, N_MOD * D_MODEL) and w_in.shape == (1, D_MODEL, D_IN)

    c_all = jnp.concatenate([c, c_ctx[None, :], jnp.zeros((MOD_ROWS - BATCH - 1, D_MODEL), F32)], axis=0)
    w_in_b = w_in[0].astype(BF16)
    w_dt = jnp.pad(w_in_b[:, COL_DT:], ((0, 0), (0, LANES - 2 * SSD_HEADS)))
    dt_bias = jnp.pad(ssd_dt_bias[0].reshape(1, 2 * SSD_HEADS), ((0, 0), (0, LANES - 2 * SSD_HEADS)))
    a_pad = jnp.pad(-jnp.exp(ssd_a_log[0].astype(F32)).reshape(1, 2 * SSD_HEADS),
                    ((0, 0), (0, LANES - 2 * SSD_HEADS)))
    lbs = jnp.cumsum(jax.nn.softmax(hgrn_lb_raw.astype(F32), axis=0), axis=0)[0][:, None, :]
    d_skip = jnp.repeat(ssd_d[0], SSD_HEADDIM)[None, :]

    mod = _ada(c_all, w_ada, b_ada)
    mod3 = mod.reshape(MOD_ROWS, 1, N_MOD * D_MODEL)
    q, f, v, g, z, xs, dtx_f, dtx_b, bc, dt = _in_proj(
        x, ctx, mod3, norm_mix, w_in_b, w_dt, conv_w[0], conv_b, dt_bias)
    o_f, y_f = _fwd_scan(q, f, v, dtx_f, bc, dt, lbs, a_pad)
    ymix = _bwd_scan(q, f, v, dtx_b, xs, bc, dt, g, z, o_f, y_f, lbs, a_pad,
                     hgrn_norm, d_skip, ssd_norm)
    return _out_ffn(ymix, x, mod3, w_out[0].astype(BF16), norm_ffn, w_gate[0].astype(BF16),
                    w_up[0].astype(BF16), w_down[0].astype(BF16), final_norm[None, :])
```

```python
import jax
import jax.numpy as jnp
from jax import lax
from jax.experimental import pallas as pl
from jax.experimental.pallas import tpu as pltpu

F32 = jnp.float32
BF16 = jnp.bfloat16

D_MODEL = 1024
BATCH = 8
SEQ = 4096
CTX_LEN = 256
GRID_W = 64
HG_HEADS = 8
HG_F = 128
HG_CHUNK = 64
SSD_HEADS = 16
SSD_HEADDIM = 64
SSD_GROUPS = 4
SSD_HPG = SSD_HEADS // SSD_GROUPS
SSD_STATE = 128
SSD_CONV = 5
SSD_CHUNK = 128
SSD_WIDTH = SSD_HEADS * SSD_HEADDIM
SSD_BC = SSD_GROUPS * SSD_STATE
D_FF = 2816
N_MOD = 6
EPS = 1e-6
LOG2E = 1.4426950408889634

LANES = 128
SUBLANES = 8

TILE = 256
N_LAT_TILES = SEQ // TILE
N_TILES = N_LAT_TILES + 1
MOD_ROWS = 16
CTX_ROW = BATCH
HG_NC = TILE // HG_CHUNK
SSD_NC = TILE // SSD_CHUNK
HG_PAIR = 2 * HG_F
SSD_GRP_W = SSD_HPG * SSD_HEADDIM

COL_Q = 0
COL_F = 1024
COL_I = 3072
COL_G = 4096
COL_Z = 5120
COL_XBC = 6144
COL_DT = 8192
D_IN = 8224
PROJ_CHUNK = 512
CONV_HALO = SUBLANES
FFN_CHUNK = 256
FFN_TILE = 2 * TILE

VMEM_LIMIT = 56 * 1024 * 1024


def _silu(x):
    return x * jax.nn.sigmoid(x)


def _rms(x, w):
    return x * lax.rsqrt(jnp.mean(x * x, axis=-1, keepdims=True) + EPS) * w


def _dot(a, b):
    return jnp.dot(a, b, preferred_element_type=F32)


def _dot_nt(a, b):
    return lax.dot_general(a, b, (((1,), (1,)), ((), ())), preferred_element_type=F32)


def _dot_tn(a, b):
    return lax.dot_general(a, b, (((0,), (0,)), ((), ())), preferred_element_type=F32)


def _cumsum_mm(tri_b, x):
    hi = x.astype(BF16)
    r1 = x - hi.astype(F32)
    mid = r1.astype(BF16)
    lo = (r1 - mid.astype(F32)).astype(BF16)
    return _dot(tri_b, hi) + _dot(tri_b, mid) + _dot(tri_b, lo)


def _block_diag(a, b):
    z = jnp.zeros_like(a)
    return jnp.concatenate([jnp.concatenate([a, z], axis=1), jnp.concatenate([z, b], axis=1)], axis=0)


def _lane_pair(col_a, col_b):
    n = col_a.shape[0]
    lane = lax.broadcasted_iota(jnp.int32, (n, LANES), 1)
    return jnp.where(lane < SSD_HEADDIM, col_a, col_b)


def _const_spec(shape):
    nd = len(shape)
    return pl.BlockSpec(shape, lambda *_: (0,) * nd, pipeline_mode=pl.Buffered(1))


def _ada_kernel(c_ref, w_ref, b_ref, o_ref):
    act = _silu(c_ref[...]).astype(BF16)
    o_ref[...] = _dot(act, w_ref[...].astype(BF16)) + b_ref[...]


def _ada(c_all, w_ada, b_ada):
    n = N_MOD * D_MODEL
    bn = 1536
    return pl.pallas_call(
        _ada_kernel,
        grid=(n // bn,),
        in_specs=[
            pl.BlockSpec((MOD_ROWS, D_MODEL), lambda j: (0, 0)),
            pl.BlockSpec((None, D_MODEL, bn), lambda j: (0, 0, j)),
            pl.BlockSpec((1, bn), lambda j: (0, j)),
        ],
        out_specs=pl.BlockSpec((MOD_ROWS, bn), lambda j: (0, j)),
        out_shape=jax.ShapeDtypeStruct((MOD_ROWS, n), F32),
        compiler_params=pltpu.CompilerParams(vmem_limit_bytes=VMEM_LIMIT),
        name="ada",
    )(c_all, w_ada, b_ada)


def _in_proj_kernel(x_ref, ctx_ref, mod_ref, nw_ref, w_ref, wdt_ref, cw_ref, cb_ref, dtb_ref,
                    q_ref, f_ref, v_ref, g_ref, z_ref, xs_ref, dxf_ref, dxb_ref, bc_ref, dt_ref,
                    u_s, xbc_s):
    t = pl.program_id(1)
    is_ctx = t == 0
    x = jnp.where(is_ctx, ctx_ref[...], x_ref[...])
    shift = mod_ref[:, 0:D_MODEL]
    scale = mod_ref[:, D_MODEL:2 * D_MODEL]
    u_s[...] = (_rms(x, nw_ref[...] * (1.0 + scale)) + shift).astype(BF16)

    def proj(c0, width):
        return _dot(u_s[...], w_ref[:, c0:c0 + width])

    dt = jax.nn.softplus(_dot(u_s[...], wdt_ref[...]) + dtb_ref[...])
    dt_ref[...] = dt
    zero_halo = jnp.zeros((CONV_HALO, SSD_WIDTH + 2 * SSD_BC), F32)
    xbc_s[0:CONV_HALO, :] = zero_halo
    xbc_s[CONV_HALO + TILE:, :] = zero_halo
    for c0 in range(COL_XBC, COL_DT, PROJ_CHUNK):
        xbc_s[CONV_HALO:CONV_HALO + TILE, c0 - COL_XBC:c0 - COL_XBC + PROJ_CHUNK] = proj(c0, PROJ_CHUNK)

    def proj_chunk(c0):
        y = proj(c0, PROJ_CHUNK)
        if c0 < COL_F:
            q_ref[:, c0 - COL_Q:c0 - COL_Q + PROJ_CHUNK] = (_silu(y) * (HG_F ** -0.5)).astype(BF16)
        elif c0 < COL_I:
            f_ref[:, c0 - COL_F:c0 - COL_F + PROJ_CHUNK] = y
        elif c0 < COL_G:
            v_ref[:, c0 - COL_I:c0 - COL_I + PROJ_CHUNK] = y.astype(BF16)
        elif c0 < COL_Z:
            g_ref[:, c0 - COL_G:c0 - COL_G + PROJ_CHUNK] = _silu(y).astype(BF16)
        else:
            z_ref[:, c0 - COL_Z:c0 - COL_Z + PROJ_CHUNK] = _silu(y).astype(BF16)

    pad = SSD_CONV // 2

    def conv_slab(j):
        cols = slice(j * LANES, (j + 1) * LANES)
        blk_rows = GRID_W + 2 * CONV_HALO
        for r0 in range(0, TILE, GRID_W):
            rows = slice(r0, r0 + GRID_W)
            blk = xbc_s[r0:r0 + blk_rows, cols]
            below = jnp.where(is_ctx, blk[:CONV_HALO], 0.0)
            above = jnp.where(is_ctx, blk[CONV_HALO + GRID_W:], 0.0)
            blk = jnp.concatenate([below, blk[CONV_HALO:CONV_HALO + GRID_W], above], axis=0)
            acc = jnp.broadcast_to(cb_ref[:, cols], (GRID_W, LANES))
            for k in range(SSD_CONV):
                win = blk if k == pad else pltpu.roll(blk, (pad - k) % blk_rows, axis=0)
                acc = acc + cw_ref[k:k + 1, cols] * win[CONV_HALO:CONV_HALO + GRID_W]
            act = _silu(acc)
            if j < SSD_WIDTH // LANES:
                xs_ref[rows, cols] = act.astype(BF16)
                for dref, h0 in ((dxf_ref, 2 * j), (dxb_ref, SSD_HEADS + 2 * j)):
                    dref[rows, cols] = (_lane_pair(dt[rows, h0:h0 + 1], dt[rows, h0 + 1:h0 + 2]) * act).astype(BF16)
            else:
                bc_ref[rows, j * LANES - SSD_WIDTH:(j + 1) * LANES - SSD_WIDTH] = act.astype(BF16)

    for j in range((SSD_WIDTH + 2 * SSD_BC) // LANES):
        conv_slab(j)
    for c0 in (list(range(COL_F, COL_G, PROJ_CHUNK)) + list(range(COL_Q, COL_F, PROJ_CHUNK))
               + list(range(COL_G, COL_XBC, PROJ_CHUNK))):
        proj_chunk(c0)


def _in_proj(x, ctx, mod3, norm_w, w_in, w_dt, conv_w, conv_b, dt_bias):
    lat = lambda b, t: (b, jnp.maximum(t - 1, 0), 0)
    tok = lambda b, t: (b, t, 0)
    n_tok = N_TILES * TILE

    def out(width, dtype):
        return jax.ShapeDtypeStruct((BATCH, n_tok, width), dtype), pl.BlockSpec((None, TILE, width), tok)

    outs = [out(D_MODEL, BF16), out(2 * D_MODEL, F32), out(D_MODEL, BF16), out(D_MODEL, BF16),
            out(D_MODEL, BF16), out(SSD_WIDTH, BF16), out(SSD_WIDTH, BF16), out(SSD_WIDTH, BF16),
            out(2 * SSD_BC, BF16), out(LANES, F32)]
    return pl.pallas_call(
        _in_proj_kernel,
        grid=(BATCH, N_TILES),
        in_specs=[
            pl.BlockSpec((None, TILE, D_MODEL), lat),
            pl.BlockSpec((None, CTX_LEN, D_MODEL), lambda b, t: (b, 0, 0)),
            pl.BlockSpec((None, 1, N_MOD * D_MODEL), lambda b, t: (jnp.where(t == 0, CTX_ROW, b), 0, 0)),
            _const_spec((1, D_MODEL)),
            _const_spec((D_MODEL, D_IN)),
            _const_spec((D_MODEL, LANES)),
            _const_spec((SSD_CONV, SSD_WIDTH + 2 * SSD_BC)),
            _const_spec((1, SSD_WIDTH + 2 * SSD_BC)),
            _const_spec((1, LANES)),
        ],
        out_specs=[o[1] for o in outs],
        out_shape=[o[0] for o in outs],
        scratch_shapes=[
            pltpu.VMEM((TILE, D_MODEL), BF16),
            pltpu.VMEM((TILE + 2 * CONV_HALO, SSD_WIDTH + 2 * SSD_BC), F32),
        ],
        compiler_params=pltpu.CompilerParams(
            dimension_semantics=("parallel", "arbitrary"), vmem_limit_bytes=VMEM_LIMIT),
        name="in_proj",
    )(x, ctx, mod3, norm_w, w_in, w_dt, conv_w, conv_b, dt_bias)


def _tri(n, reverse):
    i = lax.broadcasted_iota(jnp.int32, (n, n), 0)
    j = lax.broadcasted_iota(jnp.int32, (n, n), 1)
    return (j >= i) if reverse else (j <= i)


def _hgrn_stages(q_ref, f_ref, v_ref, lb_ref, s_ref, qd_s, ki_s, ke_s, sc_s, sp_s, et_s, reverse, emit):
    mask = _tri(HG_CHUNK, reverse)
    tri_b = jnp.where(mask, 1.0, 0.0).astype(BF16)
    i2 = lax.broadcasted_iota(jnp.int32, (HG_CHUNK, LANES), 0)
    j2 = lax.broadcasted_iota(jnp.int32, (HG_CHUNK, LANES), 1) & (HG_CHUNK - 1)
    mask2 = (j2 >= i2) if reverse else (j2 <= i2)
    last = slice(0, 1) if reverse else slice(HG_CHUNK - 1, HG_CHUNK)
    chunk_rows = [slice(c * HG_CHUNK, (c + 1) * HG_CHUNK) for c in range(HG_NC)]
    pair_cols = [slice(p * HG_PAIR, (p + 1) * HG_PAIR) for p in range(HG_HEADS // 2)]
    order = list(reversed(range(HG_NC))) if reverse else list(range(HG_NC))

    def gates(chunks=range(HG_NC)):
        for c in chunks:
            rows = chunk_rows[c]
            for cols in pair_cols:
                lb = lb_ref[:, cols]
                f = lb + (1.0 - lb) * jax.nn.sigmoid(f_ref[rows, cols])
                k = 1.0 - f
                cum = _cumsum_mm(tri_b, jnp.log(f)) * LOG2E
                e_cum = jnp.exp2(cum)
                e_tot = e_cum[last, :]
                k_inv = k / e_cum
                qd_s[rows, cols] = (q_ref[rows, cols] * e_cum).astype(BF16)
                ki_s[rows, cols] = k_inv.astype(BF16)
                ke_s[rows, cols] = (k_inv * e_tot).astype(BF16)
                et_s[c * SUBLANES:c * SUBLANES + 1, cols] = e_tot

    def scores(chunks=range(HG_NC)):
        for c in chunks:
            rows = chunk_rows[c]
            for p, cols in enumerate(pair_cols):
                kp = ki_s[rows, cols]
                sc = _dot_nt(qd_s[rows, cols], _block_diag(kp[:, :HG_F], kp[:, HG_F:]))
                sc_s[rows, p * LANES:(p + 1) * LANES] = jnp.where(mask2, sc, 0.0).astype(BF16)

    def states():
        for h in range(HG_HEADS):
            cols = slice(h * HG_F, (h + 1) * HG_F)
            s = s_ref[h]
            for c in order:
                rows = chunk_rows[c]
                sp_s[c, h] = s.astype(BF16)
                s = s * et_s[c * SUBLANES:c * SUBLANES + 1, cols] + _dot_tn(v_ref[rows, cols], ke_s[rows, cols])
            s_ref[h] = s

    def outputs():
        for c in range(HG_NC):
            rows = chunk_rows[c]
            for p, cols in enumerate(pair_cols):
                vp = v_ref[rows, cols]
                o = _dot(sc_s[rows, p * LANES:(p + 1) * LANES], _block_diag(vp[:, :HG_F], vp[:, HG_F:]))
                o = o + _dot_nt(qd_s[rows, cols], _block_diag(sp_s[c, 2 * p], sp_s[c, 2 * p + 1]))
                emit(rows, p, o)

    return [gates, scores, states, outputs]


def _ssd_stages(dtx_ref, bc_ref, dt_ref, a_ref, s_ref, cum_s, cumt_s, sc_s, m_s, dxe_s, ecum_s, sp_s,
                reverse, dcol, emit):
    mask = _tri(SSD_CHUNK, reverse)
    tri_b = jnp.where(mask, 1.0, 0.0).astype(BF16)
    last = slice(0, 1) if reverse else slice(SSD_CHUNK - 1, SSD_CHUNK)
    chunk_rows = [slice(c * SSD_CHUNK, (c + 1) * SSD_CHUNK) for c in range(SSD_NC)]
    order = list(reversed(range(SSD_NC))) if reverse else list(range(SSD_NC))
    lane = lax.broadcasted_iota(jnp.int32, (SSD_CHUNK, LANES), 1)
    low = lane < SSD_HEADDIM

    def decays():
        for c in range(SSD_NC):
            cum = _cumsum_mm(tri_b, dt_ref[chunk_rows[c], :] * (a_ref[...] * LOG2E))
            cum_s[c] = cum
            cumt_s[c] = cum.T

    def scores():
        for c in range(SSD_NC):
            rows = chunk_rows[c]
            for g in range(SSD_GROUPS):
                bg = bc_ref[rows, g * SSD_STATE:(g + 1) * SSD_STATE]
                cg = bc_ref[rows, SSD_BC + g * SSD_STATE:SSD_BC + (g + 1) * SSD_STATE]
                sc_s[c, g] = _dot_nt(cg, bg)

    def masks(chunks=range(SSD_NC)):
        for c in chunks:
            rows = chunk_rows[c]
            cum = cum_s[c]
            cum_t = cumt_s[c]
            for pair in range(SSD_HEADS // 2):
                rs, tots = [], []
                for h in (2 * pair, 2 * pair + 1):
                    col = dcol + h
                    r = jnp.broadcast_to(cum[:, col:col + 1], (SSD_CHUNK, SSD_CHUNK))
                    row_t = cum_t[col:col + 1, :]
                    decay = jnp.exp2(jnp.where(mask, r - row_t, -jnp.inf))
                    m_s[c, h] = (sc_s[c, h // SSD_HPG] * decay).astype(BF16)
                    rs.append(r)
                    tots.append(row_t[:, last.start:last.stop])
                r2 = jnp.where(low, rs[0], rs[1])
                tot2 = jnp.where(low[0:1, :], tots[0], tots[1])
                cols = slice(pair * LANES, (pair + 1) * LANES)
                ecum_s[rows, cols] = jnp.exp2(r2)
                dxe_s[rows, cols] = (jnp.exp2(tot2 - r2) * dtx_ref[rows, cols].astype(F32)).astype(BF16)

    def states():
        for g in range(SSD_GROUPS):
            grp = slice(g * SSD_GRP_W, (g + 1) * SSD_GRP_W)
            s = s_ref[grp, :]
            for c in order:
                rows = chunk_rows[c]
                sp_s[c, grp, :] = s.astype(BF16)
                ds = _dot_tn(dxe_s[rows, grp], bc_ref[rows, g * SSD_STATE:(g + 1) * SSD_STATE])
                tot_t = cumt_s[c][:, last.start:last.stop]
                dec = jnp.concatenate(
                    [jnp.broadcast_to(jnp.exp2(tot_t[dcol + g * SSD_HPG + hh:dcol + g * SSD_HPG + hh + 1, :]),
                                      (SSD_HEADDIM, SSD_STATE)) for hh in range(SSD_HPG)], axis=0)
                s = s * dec + ds
            s_ref[grp, :] = s

    def outputs():
        for c in range(SSD_NC):
            rows = chunk_rows[c]
            for g in range(SSD_GROUPS):
                grp = slice(g * SSD_GRP_W, (g + 1) * SSD_GRP_W)
                cg = bc_ref[rows, SSD_BC + g * SSD_STATE:SSD_BC + (g + 1) * SSD_STATE]
                y_inter = _dot_nt(cg, sp_s[c, grp, :])
                ys = []
                for pp in range(SSD_HPG // 2):
                    pair = g * (SSD_HPG // 2) + pp
                    cols = slice(pair * LANES, (pair + 1) * LANES)
                    dp = dtx_ref[rows, cols]
                    zero = jnp.zeros_like(dp)
                    rhs = jnp.concatenate([jnp.where(low, dp, zero), jnp.where(low, zero, dp)], axis=0)
                    y_intra = _dot(jnp.concatenate([m_s[c, 2 * pair], m_s[c, 2 * pair + 1]], axis=1), rhs)
                    ys.append(y_intra + ecum_s[rows, cols] * y_inter[:, pp * LANES:(pp + 1) * LANES])
                emit(rows, g, ys)

    return [decays, scores, masks, states, outputs]


def _run_scan_stages(hg, sd):
    gates, hg_scores, hg_states, hg_outputs = hg
    decays, sd_scores, masks, sd_states, sd_outputs = sd
    decays()
    gates([0])
    sd_scores()
    hg_scores([0])
    gates([1])
    masks([0])
    hg_scores([1])
    gates([2])
    masks([1])
    hg_scores([2])
    gates([3])
    hg_scores([3])
    hg_states()
    sd_states()
    hg_outputs()
    sd_outputs()


_SCAN_SCRATCH = [
    pltpu.VMEM((HG_HEADS, HG_F, HG_F), F32),
    pltpu.VMEM((SSD_WIDTH, SSD_STATE), F32),
    pltpu.VMEM((TILE, D_MODEL), BF16),
    pltpu.VMEM((TILE, D_MODEL), BF16),
    pltpu.VMEM((TILE, D_MODEL), BF16),
    pltpu.VMEM((TILE, HG_HEADS // 2 * LANES), BF16),
    pltpu.VMEM((HG_NC, HG_HEADS, HG_F, HG_F), BF16),
    pltpu.VMEM((HG_NC * SUBLANES, D_MODEL), F32),
    pltpu.VMEM((SSD_NC, SSD_CHUNK, LANES), F32),
    pltpu.VMEM((SSD_NC, LANES, SSD_CHUNK), F32),
    pltpu.VMEM((SSD_NC, SSD_GROUPS, SSD_CHUNK, SSD_CHUNK), F32),
    pltpu.VMEM((SSD_NC, SSD_HEADS, SSD_CHUNK, SSD_CHUNK), BF16),
    pltpu.VMEM((TILE, SSD_WIDTH), BF16),
    pltpu.VMEM((TILE, SSD_WIDTH), F32),
    pltpu.VMEM((SSD_NC, SSD_WIDTH, SSD_STATE), BF16),
]


def _fwd_scan_kernel(q_ref, f_ref, v_ref, dtx_ref, bc_ref, dt_ref, lb_ref, a_ref,
                     o_ref, y_ref, shg_s, sssd_s, qd_s, ki_s, ke_s, hsc_s, hsp_s, et_s,
                     cum_s, cumt_s, ssc_s, m_s, dxe_s, ecum_s, ssp_s):
    t = pl.program_id(1)

    @pl.when(t == 0)
    def _():
        shg_s[...] = jnp.zeros_like(shg_s)
        sssd_s[...] = jnp.zeros_like(sssd_s)

    def emit_hg(rows, p, o):
        o_ref[rows, p * HG_PAIR:(p + 1) * HG_PAIR] = o.astype(BF16)

    def emit_ssd(rows, g, ys):
        for i, y in enumerate(ys):
            y_ref[rows, g * SSD_GRP_W + i * LANES:g * SSD_GRP_W + (i + 1) * LANES] = y.astype(BF16)

    hg = _hgrn_stages(q_ref, f_ref, v_ref, lb_ref, shg_s, qd_s, ki_s, ke_s, hsc_s, hsp_s, et_s, False, emit_hg)
    sd = _ssd_stages(dtx_ref, bc_ref, dt_ref, a_ref, sssd_s, cum_s, cumt_s, ssc_s, m_s, dxe_s, ecum_s, ssp_s,
                     False, 0, emit_ssd)
    _run_scan_stages(hg, sd)


def _fwd_scan(q, f, v, dtx, bc, dt, lb, a_pad):
    tok = lambda b, t: (b, t, 0)
    lat = lambda b, t: (b, jnp.maximum(t - 1, 0), 0)
    return pl.pallas_call(
        _fwd_scan_kernel,
        grid=(BATCH, N_TILES),
        in_specs=[
            pl.BlockSpec((None, TILE, D_MODEL), tok),
            pl.BlockSpec((None, TILE, D_MODEL), tok),
            pl.BlockSpec((None, TILE, D_MODEL), tok),
            pl.BlockSpec((None, TILE, SSD_WIDTH), tok),
            pl.BlockSpec((None, TILE, 2 * SSD_BC), tok),
            pl.BlockSpec((None, TILE, LANES), tok),
            pl.BlockSpec((None, 1, D_MODEL), lambda b, t: (0, 0, 0)),
            _const_spec((1, LANES)),
        ],
        out_specs=[pl.BlockSpec((None, TILE, D_MODEL), lat), pl.BlockSpec((None, TILE, SSD_WIDTH), lat)],
        out_shape=[jax.ShapeDtypeStruct((BATCH, SEQ, D_MODEL), BF16),
                   jax.ShapeDtypeStruct((BATCH, SEQ, SSD_WIDTH), BF16)],
        scratch_shapes=_SCAN_SCRATCH,
        compiler_params=pltpu.CompilerParams(
            dimension_semantics=("parallel", "arbitrary"), vmem_limit_bytes=VMEM_LIMIT),
        name="fwd_scan",
    )(q, f, v, dtx, bc, dt, lb, a_pad)


def _bwd_scan_kernel(q_ref, f_ref, v_ref, dtx_ref, xs_ref, bc_ref, dt_ref, g_ref, z_ref, of_ref, yf_ref,
                     lb_ref, a_ref, hnw_ref, dsk_ref, snw_ref, out_ref,
                     shg_s, sssd_s, qd_s, ki_s, ke_s, hsc_s, hsp_s, et_s,
                     cum_s, cumt_s, ssc_s, m_s, dxe_s, ecum_s, ssp_s):
    s = pl.program_id(1)

    @pl.when(s == 0)
    def _():
        shg_s[...] = jnp.zeros_like(shg_s)
        sssd_s[...] = jnp.zeros_like(sssd_s)

    def emit_hg(rows, p, o_b):
        for i in range(2):
            cols = slice(p * HG_PAIR + i * HG_F, p * HG_PAIR + (i + 1) * HG_F)
            o = of_ref[rows, cols] + o_b[:, i * HG_F:(i + 1) * HG_F]
            out_ref[rows, cols] = (_rms(o, hnw_ref[...]) * g_ref[rows, cols]).astype(BF16)

    def emit_ssd(rows, g, ys):
        us = []
        for i, y_b in enumerate(ys):
            cols = slice(g * SSD_GRP_W + i * LANES, g * SSD_GRP_W + (i + 1) * LANES)
            y = yf_ref[rows, cols] + y_b + dsk_ref[:, cols] * xs_ref[rows, cols]
            us.append(y * z_ref[rows, cols])
        ms = sum(jnp.sum(u * u, axis=-1, keepdims=True) for u in us) / SSD_GRP_W
        inv = lax.rsqrt(ms + EPS)
        for i, u in enumerate(us):
            cols = slice(g * SSD_GRP_W + i * LANES, g * SSD_GRP_W + (i + 1) * LANES)
            out_ref[rows, SSD_WIDTH + cols.start:SSD_WIDTH + cols.stop] = (
                u * inv * snw_ref[:, cols]).astype(BF16)

    hg = _hgrn_stages(q_ref, f_ref, v_ref, lb_ref, shg_s, qd_s, ki_s, ke_s, hsc_s, hsp_s, et_s, True, emit_hg)
    sd = _ssd_stages(dtx_ref, bc_ref, dt_ref, a_ref, sssd_s, cum_s, cumt_s, ssc_s, m_s, dxe_s, ecum_s, ssp_s,
                     True, SSD_HEADS, emit_ssd)
    _run_scan_stages(hg, sd)


def _bwd_scan(q, f, v, dtx, xs, bc, dt, g, z, o_f, y_f, lb, a_pad, hg_norm, d_skip, ssd_norm):
    tok = lambda b, s: (b, jnp.where(s == 0, 0, N_TILES - s), 0)
    fcol = lambda b, s: (b, jnp.where(s == 0, 0, N_TILES - s), 1)
    lat = lambda b, s: (b, jnp.where(s == 0, N_LAT_TILES - 1, N_LAT_TILES - s), 0)
    return pl.pallas_call(
        _bwd_scan_kernel,
        grid=(BATCH, N_TILES),
        in_specs=[
            pl.BlockSpec((None, TILE, D_MODEL), tok),
            pl.BlockSpec((None, TILE, D_MODEL), fcol),
            pl.BlockSpec((None, TILE, D_MODEL), tok),
            pl.BlockSpec((None, TILE, SSD_WIDTH), tok),
            pl.BlockSpec((None, TILE, SSD_WIDTH), tok),
            pl.BlockSpec((None, TILE, 2 * SSD_BC), tok),
            pl.BlockSpec((None, TILE, LANES), tok),
            pl.BlockSpec((None, TILE, D_MODEL), tok),
            pl.BlockSpec((None, TILE, SSD_WIDTH), tok),
            pl.BlockSpec((None, TILE, D_MODEL), lat),
            pl.BlockSpec((None, TILE, SSD_WIDTH), lat),
            pl.BlockSpec((None, 1, D_MODEL), lambda b, s: (1, 0, 0)),
            _const_spec((1, LANES)),
            _const_spec((1, HG_F)),
            _const_spec((1, SSD_WIDTH)),
            _const_spec((1, SSD_WIDTH)),
        ],
        out_specs=pl.BlockSpec((None, TILE, 2 * D_MODEL), lat),
        out_shape=jax.ShapeDtypeStruct((BATCH, SEQ, 2 * D_MODEL), BF16),
        scratch_shapes=_SCAN_SCRATCH,
        compiler_params=pltpu.CompilerParams(
            dimension_semantics=("parallel", "arbitrary"), vmem_limit_bytes=VMEM_LIMIT),
        name="bwd_scan",
    )(q, f, v, dtx, xs, bc, dt, g, z, o_f, y_f, lb, a_pad, hg_norm, d_skip, ssd_norm)


def _out_ffn_kernel(y_ref, x_ref, mod_ref, wo_ref, nf_ref, wg_ref, wu_ref, wd_ref, fn_ref,
                    out_ref, u_s, h_s, acc_s):
    g1 = mod_ref[:, 2 * D_MODEL:3 * D_MODEL]
    sh2 = mod_ref[:, 3 * D_MODEL:4 * D_MODEL]
    sc2 = mod_ref[:, 4 * D_MODEL:5 * D_MODEL]
    g2 = mod_ref[:, 5 * D_MODEL:6 * D_MODEL]
    halves = [slice(i * TILE, (i + 1) * TILE) for i in range(FFN_TILE // TILE)]
    n_chunks = D_FF // FFN_CHUNK

    for r in halves:
        h = x_ref[r, :] + g1 * _dot(y_ref[r, :], wo_ref[...])
        h_s[r, :] = h
        u_s[r, :] = (_rms(h, nf_ref[...]) * (1.0 + sc2) + sh2).astype(BF16)

    for r in halves:
        def gate_up(c):
            cols = slice(c * FFN_CHUNK, (c + 1) * FFN_CHUNK)
            return _dot(u_s[r, :], wg_ref[:, cols]), _dot(u_s[r, :], wu_ref[:, cols])

        cur = gate_up(0)
        for c in range(n_chunks):
            nxt = gate_up(c + 1) if c + 1 < n_chunks else None
            part = _dot((_silu(cur[0]) * cur[1]).astype(BF16), wd_ref[c * FFN_CHUNK:(c + 1) * FFN_CHUNK, :])
            if c == 0:
                acc_s[r, :] = part
            else:
                acc_s[r, :] += part
            cur = nxt
        out_ref[r, :] = _rms(h_s[r, :] + g2 * acc_s[r, :], fn_ref[...])


def _out_ffn(ymix, x, mod3, w_out, norm_ffn, w_gate, w_up, w_down, final_norm):
    tok = lambda b, t: (b, t, 0)
    return pl.pallas_call(
        _out_ffn_kernel,
        grid=(BATCH, SEQ // FFN_TILE),
        in_specs=[
            pl.BlockSpec((None, FFN_TILE, 2 * D_MODEL), tok),
            pl.BlockSpec((None, FFN_TILE, D_MODEL), tok),
            pl.BlockSpec((None, 1, N_MOD * D_MODEL), lambda b, t: (b, 0, 0)),
            _const_spec((2 * D_MODEL, D_MODEL)),
            _const_spec((1, D_MODEL)),
            _const_spec((D_MODEL, D_FF)),
            _const_spec((D_MODEL, D_FF)),
            _const_spec((D_FF, D_MODEL)),
            _const_spec((1, D_MODEL)),
        ],
        out_specs=pl.BlockSpec((None, FFN_TILE, D_MODEL), tok),
        out_shape=jax.ShapeDtypeStruct((BATCH, SEQ, D_MODEL), F32),
        scratch_shapes=[pltpu.VMEM((FFN_TILE, D_MODEL), BF16), pltpu.VMEM((FFN_TILE, D_MODEL), F32),
                        pltpu.VMEM((FFN_TILE, D_MODEL), F32)],
        compiler_params=pltpu.CompilerParams(
            dimension_semantics=("parallel", "arbitrary"), vmem_limit_bytes=VMEM_LIMIT),
        name="out_ffn",
    )(ymix, x, mod3, w_out, norm_ffn, w_gate, w_up, w_down, final_norm)


def kernel(x, c, ctx, c_ctx, w_ada, b_ada, norm_mix, w_in, conv_w, conv_b, ssd_a_log, ssd_dt_bias,
           ssd_d, ssd_norm, hgrn_lb_raw, hgrn_norm, w_out, norm_ffn, w_gate, w_up, w_down, final_norm):
    assert x.shape == (BATCH, SEQ, D_MODEL) and ctx.shape == (BATCH, CTX_LEN, D_MODEL)
    assert w_ada.shape == (1, D_MODEL, N_MOD * D_MODEL) and w_in.shape == (1, D_MODEL, D_IN)

    c_all = jnp.concatenate([c, c_ctx[None, :], jnp.zeros((MOD_ROWS - BATCH - 1, D_MODEL), F32)], axis=0)
    w_in_b = w_in[0].astype(BF16)
    w_dt = jnp.pad(w_in_b[:, COL_DT:], ((0, 0), (0, LANES - 2 * SSD_HEADS)))
    dt_bias = jnp.pad(ssd_dt_bias[0].reshape(1, 2 * SSD_HEADS), ((0, 0), (0, LANES - 2 * SSD_HEADS)))
    a_pad = jnp.pad(-jnp.exp(ssd_a_log[0].astype(F32)).reshape(1, 2 * SSD_HEADS),
                    ((0, 0), (0, LANES - 2 * SSD_HEADS)))
    lbs = jnp.cumsum(jax.nn.softmax(hgrn_lb_raw.astype(F32), axis=0), axis=0)[0][:, None, :]
    d_skip = jnp.repeat(ssd_d[0], SSD_HEADDIM)[None, :]

    mod = _ada(c_all, w_ada, b_ada)
    mod3 = mod.reshape(MOD_ROWS, 1, N_MOD * D_MODEL)
    q, f, v, g, z, xs, dtx_f, dtx_b, bc, dt = _in_proj(
        x, ctx, mod3, norm_mix, w_in_b, w_dt, conv_w[0], conv_b, dt_bias)
    o_f, y_f = _fwd_scan(q, f, v, dtx_f, bc, dt, lbs, a_pad)
    ymix = _bwd_scan(q, f, v, dtx_b, xs, bc, dt, g, z, o_f, y_f, lbs, a_pad,
                     hgrn_norm, d_skip, ssd_norm)
    return _out_ffn(ymix, x, mod3, w_out[0].astype(BF16), norm_ffn, w_gate[0].astype(BF16),
                    w_up[0].astype(BF16), w_down[0].astype(BF16), final_norm[None, :])
```

```python
import jax
import jax.numpy as jnp
from jax import lax
from jax.experimental import pallas as pl
from jax.experimental.pallas import tpu as pltpu

F32 = jnp.float32
BF16 = jnp.bfloat16

D_MODEL = 1024
BATCH = 8
SEQ = 4096
CTX_LEN = 256
GRID_W = 64
HG_HEADS = 8
HG_F = 128
HG_CHUNK = 64
SSD_HEADS = 16
SSD_HEADDIM = 64
SSD_GROUPS = 4
SSD_HPG = SSD_HEADS // SSD_GROUPS
SSD_STATE = 128
SSD_CONV = 5
SSD_CHUNK = 128
SSD_WIDTH = SSD_HEADS * SSD_HEADDIM
SSD_BC = SSD_GROUPS * SSD_STATE
D_FF = 2816
N_MOD = 6
EPS = 1e-6
LOG2E = 1.4426950408889634

LANES = 128
SUBLANES = 8

TILE = 256
N_LAT_TILES = SEQ // TILE
N_TILES = N_LAT_TILES + 1
MOD_ROWS = 16
CTX_ROW = BATCH
HG_NC = TILE // HG_CHUNK
SSD_NC = TILE // SSD_CHUNK
HG_PAIR = 2 * HG_F
SSD_GRP_W = SSD_HPG * SSD_HEADDIM

COL_Q = 0
COL_F = 1024
COL_I = 3072
COL_G = 4096
COL_Z = 5120
COL_XBC = 6144
COL_DT = 8192
D_IN = 8224
PROJ_CHUNK = 512
CONV_HALO = SUBLANES
FFN_CHUNK = 256
FFN_TILE = 2 * TILE
IN_TILES_PER_STEP = 2
SLAB_RELEASE = 0.6

VMEM_LIMIT = 56 * 1024 * 1024
IN_PROJ_VMEM_LIMIT = 60 * 1024 * 1024


def _silu(x):
    return x * jax.nn.sigmoid(x)


def _rms(x, w):
    return x * lax.rsqrt(jnp.mean(x * x, axis=-1, keepdims=True) + EPS) * w


def _dot(a, b):
    return jnp.dot(a, b, preferred_element_type=F32)


def _dot_nt(a, b):
    return lax.dot_general(a, b, (((1,), (1,)), ((), ())), preferred_element_type=F32)


def _dot_tn(a, b):
    return lax.dot_general(a, b, (((0,), (0,)), ((), ())), preferred_element_type=F32)


def _cumsum_mm(tri_b, x):
    hi = x.astype(BF16)
    r1 = x - hi.astype(F32)
    mid = r1.astype(BF16)
    lo = (r1 - mid.astype(F32)).astype(BF16)
    return _dot(tri_b, hi) + _dot(tri_b, mid) + _dot(tri_b, lo)


def _block_diag(a, b):
    z = jnp.zeros_like(a)
    return jnp.concatenate([jnp.concatenate([a, z], axis=1), jnp.concatenate([z, b], axis=1)], axis=0)


def _lane_pair(col_a, col_b):
    n = col_a.shape[0]
    lane = lax.broadcasted_iota(jnp.int32, (n, LANES), 1)
    return jnp.where(lane < SSD_HEADDIM, col_a, col_b)


def _after(dst_ref, token):
    bits = lax.bitcast_convert_type(token, jnp.uint32)
    zero = lax.bitcast_convert_type((bits >> 16) >> 16, F32)
    blk = dst_ref[0:SUBLANES, 0:LANES]
    dst_ref[0:SUBLANES, 0:LANES] = (blk.astype(F32) + zero).astype(dst_ref.dtype)


def _const_spec(shape):
    nd = len(shape)
    return pl.BlockSpec(shape, lambda *_: (0,) * nd, pipeline_mode=pl.Buffered(1))


def _ada_kernel(c_ref, w_ref, b_ref, o_ref):
    act = _silu(c_ref[...]).astype(BF16)
    o_ref[...] = _dot(act, w_ref[...].astype(BF16)) + b_ref[...]


def _ada(c_all, w_ada, b_ada):
    n = N_MOD * D_MODEL
    bn = 1536
    return pl.pallas_call(
        _ada_kernel,
        grid=(n // bn,),
        in_specs=[
            pl.BlockSpec((MOD_ROWS, D_MODEL), lambda j: (0, 0)),
            pl.BlockSpec((None, D_MODEL, bn), lambda j: (0, 0, j)),
            pl.BlockSpec((1, bn), lambda j: (0, j)),
        ],
        out_specs=pl.BlockSpec((MOD_ROWS, bn), lambda j: (0, j)),
        out_shape=jax.ShapeDtypeStruct((MOD_ROWS, n), F32),
        compiler_params=pltpu.CompilerParams(vmem_limit_bytes=VMEM_LIMIT),
        name="ada",
    )(c_all, w_ada, b_ada)


def _in_proj_kernel(xa_ref, xb_ref, ca_ref, cb2_ref, moda_ref, modb_ref, nw_ref, w_ref, wdt_ref,
                    cw_ref, cb_ref, dtb_ref,
                    q_ref, f_ref, v_ref, g_ref, z_ref, xs_ref, dxf_ref, dxb_ref, bc_ref, dt_ref,
                    u_s, xbc_s):
    step = pl.program_id(0)
    pad = SSD_CONV // 2
    n_slabs = (SSD_WIDTH + 2 * SSD_BC) // LANES
    rest_cols = (list(range(COL_F, COL_G, PROJ_CHUNK)) + list(range(COL_Q, COL_F, PROJ_CHUNK))
                 + list(range(COL_G, COL_XBC, PROJ_CHUNK)))
    tiles = range(IN_TILES_PER_STEP)
    rows_of = [slice(i * TILE, (i + 1) * TILE) for i in tiles]
    is_ctx, dts = [], []

    for i, (x_ref, c_ref, mod_ref) in enumerate(((xa_ref, ca_ref, moda_ref), (xb_ref, cb2_ref, modb_ref))):
        is_ctx.append(lax.rem(IN_TILES_PER_STEP * step + i, N_TILES) == 0)
        x = jnp.where(is_ctx[i], c_ref[...], x_ref[...])
        shift = mod_ref[:, 0:D_MODEL]
        scale = mod_ref[:, D_MODEL:2 * D_MODEL]
        u_s[rows_of[i], :] = (_rms(x, nw_ref[...] * (1.0 + scale)) + shift).astype(BF16)

    def proj(i, c0, width):
        return _dot(u_s[rows_of[i], :], w_ref[:, c0:c0 + width])

    for i in tiles:
        dt = jax.nn.softplus(_dot(u_s[rows_of[i], :], wdt_ref[...]) + dtb_ref[...])
        dt_ref[rows_of[i], :] = dt
        dts.append(dt)
        zero_halo = jnp.zeros((CONV_HALO, SSD_WIDTH + 2 * SSD_BC), F32)
        xbc_s[i, 0:CONV_HALO, :] = zero_halo
        xbc_s[i, CONV_HALO + TILE:, :] = zero_halo
        for c0 in range(COL_XBC, COL_DT, PROJ_CHUNK):
            xbc_s[i, CONV_HALO:CONV_HALO + TILE, c0 - COL_XBC:c0 - COL_XBC + PROJ_CHUNK] = proj(i, c0, PROJ_CHUNK)

    def proj_chunk(i, c0):
        y = proj(i, c0, PROJ_CHUNK)
        rows = rows_of[i]
        if c0 < COL_F:
            q_ref[rows, c0 - COL_Q:c0 - COL_Q + PROJ_CHUNK] = (_silu(y) * (HG_F ** -0.5)).astype(BF16)
        elif c0 < COL_I:
            f_ref[rows, c0 - COL_F:c0 - COL_F + PROJ_CHUNK] = y
        elif c0 < COL_G:
            v_ref[rows, c0 - COL_I:c0 - COL_I + PROJ_CHUNK] = y.astype(BF16)
        elif c0 < COL_Z:
            g_ref[rows, c0 - COL_G:c0 - COL_G + PROJ_CHUNK] = _silu(y).astype(BF16)
        else:
            z_ref[rows, c0 - COL_Z:c0 - COL_Z + PROJ_CHUNK] = _silu(y).astype(BF16)
        return y[TILE - 1:TILE, PROJ_CHUNK - LANES:]

    def conv_slab(i, j):
        cols = slice(j * LANES, (j + 1) * LANES)
        blk_rows = GRID_W + 2 * CONV_HALO
        for r0 in range(0, TILE, GRID_W):
            out_rows = slice(i * TILE + r0, i * TILE + r0 + GRID_W)
            blk = xbc_s[i, r0:r0 + blk_rows, cols]
            below = jnp.where(is_ctx[i], blk[:CONV_HALO], 0.0)
            above = jnp.where(is_ctx[i], blk[CONV_HALO + GRID_W:], 0.0)
            blk = jnp.concatenate([below, blk[CONV_HALO:CONV_HALO + GRID_W], above], axis=0)
            acc = jnp.broadcast_to(cb_ref[:, cols], (GRID_W, LANES))
            for k in range(SSD_CONV):
                win = blk if k == pad else pltpu.roll(blk, (pad - k) % blk_rows, axis=0)
                acc = acc + cw_ref[k:k + 1, cols] * win[CONV_HALO:CONV_HALO + GRID_W]
            act = _silu(acc)
            if j < SSD_WIDTH // LANES:
                xs_ref[out_rows, cols] = act.astype(BF16)
                dt_blk = dts[i][r0:r0 + GRID_W]
                for dref, h0 in ((dxf_ref, 2 * j), (dxb_ref, SSD_HEADS + 2 * j)):
                    dref[out_rows, cols] = (
                        _lane_pair(dt_blk[:, h0:h0 + 1], dt_blk[:, h0 + 1:h0 + 2]) * act).astype(BF16)
            else:
                bc_ref[out_rows, j * LANES - SSD_WIDTH:(j + 1) * LANES - SSD_WIDTH] = act.astype(BF16)

    chunks = [(i, c0) for i in tiles for c0 in rest_cols]
    slabs = [(i, j) for i in tiles for j in range(n_slabs)]
    released = 0
    for ci, (i, c0) in enumerate(chunks):
        token = proj_chunk(i, c0)
        while released < len(slabs) and SLAB_RELEASE * released <= ci:
            si, j = slabs[released]
            for r0 in range(0, TILE, GRID_W):
                _after(xbc_s.at[si, CONV_HALO + r0:CONV_HALO + r0 + SUBLANES, j * LANES:(j + 1) * LANES], token)
            conv_slab(si, j)
            released += 1
    assert released == len(slabs)


def _in_proj(x, ctx, mod3, norm_w, w_in, w_dt, conv_w, conv_b, dt_bias):
    n_tok = N_TILES * TILE
    step_rows = IN_TILES_PER_STEP * TILE

    def tile_of(s, i):
        h = IN_TILES_PER_STEP * s + i
        return lax.div(h, N_TILES), lax.rem(h, N_TILES)

    def x_map(i):
        return lambda s: (tile_of(s, i)[0], jnp.maximum(tile_of(s, i)[1] - 1, 0), 0)

    def ctx_map(i):
        return lambda s: (tile_of(s, i)[0], 0, 0)

    def mod_map(i):
        return lambda s: (jnp.where(tile_of(s, i)[1] == 0, CTX_ROW, tile_of(s, i)[0]), 0, 0)

    def out(width, dtype):
        return (jax.ShapeDtypeStruct((BATCH * n_tok, width), dtype),
                pl.BlockSpec((step_rows, width), lambda s: (s, 0)))

    outs = [out(D_MODEL, BF16), out(2 * D_MODEL, F32), out(D_MODEL, BF16), out(D_MODEL, BF16),
            out(D_MODEL, BF16), out(SSD_WIDTH, BF16), out(SSD_WIDTH, BF16), out(SSD_WIDTH, BF16),
            out(2 * SSD_BC, BF16), out(LANES, F32)]
    res = pl.pallas_call(
        _in_proj_kernel,
        grid=(BATCH * N_TILES // IN_TILES_PER_STEP,),
        in_specs=[
            pl.BlockSpec((None, TILE, D_MODEL), x_map(0)),
            pl.BlockSpec((None, TILE, D_MODEL), x_map(1)),
            pl.BlockSpec((None, CTX_LEN, D_MODEL), ctx_map(0)),
            pl.BlockSpec((None, CTX_LEN, D_MODEL), ctx_map(1)),
            pl.BlockSpec((None, 1, N_MOD * D_MODEL), mod_map(0)),
            pl.BlockSpec((None, 1, N_MOD * D_MODEL), mod_map(1)),
            _const_spec((1, D_MODEL)),
            _const_spec((D_MODEL, D_IN)),
            _const_spec((D_MODEL, LANES)),
            _const_spec((SSD_CONV, SSD_WIDTH + 2 * SSD_BC)),
            _const_spec((1, SSD_WIDTH + 2 * SSD_BC)),
            _const_spec((1, LANES)),
        ],
        out_specs=[o[1] for o in outs],
        out_shape=[o[0] for o in outs],
        scratch_shapes=[
            pltpu.VMEM((step_rows, D_MODEL), BF16),
            pltpu.VMEM((IN_TILES_PER_STEP, TILE + 2 * CONV_HALO, SSD_WIDTH + 2 * SSD_BC), F32),
        ],
        compiler_params=pltpu.CompilerParams(
            dimension_semantics=("arbitrary",), vmem_limit_bytes=IN_PROJ_VMEM_LIMIT),
        name="in_proj",
    )(x, x, ctx, ctx, mod3, mod3, norm_w, w_in, w_dt, conv_w, conv_b, dt_bias)
    return [r.reshape(BATCH, n_tok, r.shape[-1]) for r in res]


def _tri(n, reverse):
    i = lax.broadcasted_iota(jnp.int32, (n, n), 0)
    j = lax.broadcasted_iota(jnp.int32, (n, n), 1)
    return (j >= i) if reverse else (j <= i)


def _hgrn_stages(q_ref, f_ref, v_ref, lb_ref, s_ref, qd_s, ki_s, ke_s, sc_s, sp_s, et_s, reverse, emit):
    mask = _tri(HG_CHUNK, reverse)
    tri_b = jnp.where(mask, 1.0, 0.0).astype(BF16)
    i2 = lax.broadcasted_iota(jnp.int32, (HG_CHUNK, LANES), 0)
    j2 = lax.broadcasted_iota(jnp.int32, (HG_CHUNK, LANES), 1) & (HG_CHUNK - 1)
    mask2 = (j2 >= i2) if reverse else (j2 <= i2)
    last = slice(0, 1) if reverse else slice(HG_CHUNK - 1, HG_CHUNK)
    chunk_rows = [slice(c * HG_CHUNK, (c + 1) * HG_CHUNK) for c in range(HG_NC)]
    pair_cols = [slice(p * HG_PAIR, (p + 1) * HG_PAIR) for p in range(HG_HEADS // 2)]
    order = list(reversed(range(HG_NC))) if reverse else list(range(HG_NC))

    def gates(chunks=range(HG_NC)):
        for c in chunks:
            rows = chunk_rows[c]
            for cols in pair_cols:
                lb = lb_ref[:, cols]
                f = lb + (1.0 - lb) * jax.nn.sigmoid(f_ref[rows, cols])
                k = 1.0 - f
                cum = _cumsum_mm(tri_b, jnp.log(f)) * LOG2E
                e_cum = jnp.exp2(cum)
                e_tot = e_cum[last, :]
                k_inv = k / e_cum
                qd_s[rows, cols] = (q_ref[rows, cols] * e_cum).astype(BF16)
                ki_s[rows, cols] = k_inv.astype(BF16)
                ke_s[rows, cols] = (k_inv * e_tot).astype(BF16)
                et_s[c * SUBLANES:c * SUBLANES + 1, cols] = e_tot

    def scores(chunks=range(HG_NC)):
        for c in chunks:
            rows = chunk_rows[c]
            for p, cols in enumerate(pair_cols):
                kp = ki_s[rows, cols]
                sc = _dot_nt(qd_s[rows, cols], _block_diag(kp[:, :HG_F], kp[:, HG_F:]))
                sc_s[rows, p * LANES:(p + 1) * LANES] = jnp.where(mask2, sc, 0.0).astype(BF16)

    def states():
        for h in range(HG_HEADS):
            cols = slice(h * HG_F, (h + 1) * HG_F)
            s = s_ref[h]
            for c in order:
                rows = chunk_rows[c]
                sp_s[c, h] = s.astype(BF16)
                s = s * et_s[c * SUBLANES:c * SUBLANES + 1, cols] + _dot_tn(v_ref[rows, cols], ke_s[rows, cols])
            s_ref[h] = s

    def outputs():
        for c in range(HG_NC):
            rows = chunk_rows[c]
            for p, cols in enumerate(pair_cols):
                vp = v_ref[rows, cols]
                o = _dot(sc_s[rows, p * LANES:(p + 1) * LANES], _block_diag(vp[:, :HG_F], vp[:, HG_F:]))
                o = o + _dot_nt(qd_s[rows, cols], _block_diag(sp_s[c, 2 * p], sp_s[c, 2 * p + 1]))
                emit(rows, p, o)

    return [gates, scores, states, outputs]


def _ssd_stages(dtx_ref, bc_ref, dt_ref, a_ref, s_ref, cum_s, cumt_s, sc_s, m_s, dxe_s, ecum_s, sp_s,
                reverse, dcol, emit):
    mask = _tri(SSD_CHUNK, reverse)
    tri_b = jnp.where(mask, 1.0, 0.0).astype(BF16)
    last = slice(0, 1) if reverse else slice(SSD_CHUNK - 1, SSD_CHUNK)
    chunk_rows = [slice(c * SSD_CHUNK, (c + 1) * SSD_CHUNK) for c in range(SSD_NC)]
    order = list(reversed(range(SSD_NC))) if reverse else list(range(SSD_NC))
    lane = lax.broadcasted_iota(jnp.int32, (SSD_CHUNK, LANES), 1)
    low = lane < SSD_HEADDIM

    def decays():
        for c in range(SSD_NC):
            cum = _cumsum_mm(tri_b, dt_ref[chunk_rows[c], :] * (a_ref[...] * LOG2E))
            cum_s[c] = cum
            cumt_s[c] = cum.T

    def scores():
        for c in range(SSD_NC):
            rows = chunk_rows[c]
            for g in range(SSD_GROUPS):
                bg = bc_ref[rows, g * SSD_STATE:(g + 1) * SSD_STATE]
                cg = bc_ref[rows, SSD_BC + g * SSD_STATE:SSD_BC + (g + 1) * SSD_STATE]
                sc_s[c, g] = _dot_nt(cg, bg)

    def masks(chunks=range(SSD_NC)):
        for c in chunks:
            rows = chunk_rows[c]
            cum = cum_s[c]
            cum_t = cumt_s[c]
            for pair in range(SSD_HEADS // 2):
                rs, tots = [], []
                for h in (2 * pair, 2 * pair + 1):
                    col = dcol + h
                    r = jnp.broadcast_to(cum[:, col:col + 1], (SSD_CHUNK, SSD_CHUNK))
                    row_t = cum_t[col:col + 1, :]
                    decay = jnp.exp2(jnp.where(mask, r - row_t, -jnp.inf))
                    m_s[c, h] = (sc_s[c, h // SSD_HPG] * decay).astype(BF16)
                    rs.append(r)
                    tots.append(row_t[:, last.start:last.stop])
                r2 = jnp.where(low, rs[0], rs[1])
                tot2 = jnp.where(low[0:1, :], tots[0], tots[1])
                cols = slice(pair * LANES, (pair + 1) * LANES)
                ecum_s[rows, cols] = jnp.exp2(r2)
                dxe_s[rows, cols] = (jnp.exp2(tot2 - r2) * dtx_ref[rows, cols].astype(F32)).astype(BF16)

    def states():
        for g in range(SSD_GROUPS):
            grp = slice(g * SSD_GRP_W, (g + 1) * SSD_GRP_W)
            s = s_ref[grp, :]
            for c in order:
                rows = chunk_rows[c]
                sp_s[c, grp, :] = s.astype(BF16)
                ds = _dot_tn(dxe_s[rows, grp], bc_ref[rows, g * SSD_STATE:(g + 1) * SSD_STATE])
                tot_t = cumt_s[c][:, last.start:last.stop]
                dec = jnp.concatenate(
                    [jnp.broadcast_to(jnp.exp2(tot_t[dcol + g * SSD_HPG + hh:dcol + g * SSD_HPG + hh + 1, :]),
                                      (SSD_HEADDIM, SSD_STATE)) for hh in range(SSD_HPG)], axis=0)
                s = s * dec + ds
            s_ref[grp, :] = s

    def outputs():
        for c in range(SSD_NC):
            rows = chunk_rows[c]
            for g in range(SSD_GROUPS):
                grp = slice(g * SSD_GRP_W, (g + 1) * SSD_GRP_W)
                cg = bc_ref[rows, SSD_BC + g * SSD_STATE:SSD_BC + (g + 1) * SSD_STATE]
                y_inter = _dot_nt(cg, sp_s[c, grp, :])
                ys = []
                for pp in range(SSD_HPG // 2):
                    pair = g * (SSD_HPG // 2) + pp
                    cols = slice(pair * LANES, (pair + 1) * LANES)
                    dp = dtx_ref[rows, cols]
                    zero = jnp.zeros_like(dp)
                    rhs = jnp.concatenate([jnp.where(low, dp, zero), jnp.where(low, zero, dp)], axis=0)
                    y_intra = _dot(jnp.concatenate([m_s[c, 2 * pair], m_s[c, 2 * pair + 1]], axis=1), rhs)
                    ys.append(y_intra + ecum_s[rows, cols] * y_inter[:, pp * LANES:(pp + 1) * LANES])
                emit(rows, g, ys)

    return [decays, scores, masks, states, outputs]


def _run_scan_stages(hg, sd):
    gates, hg_scores, hg_states, hg_outputs = hg
    decays, sd_scores, masks, sd_states, sd_outputs = sd
    decays()
    gates([0])
    sd_scores()
    hg_scores([0])
    gates([1])
    masks([0])
    hg_scores([1])
    gates([2])
    masks([1])
    hg_scores([2])
    gates([3])
    hg_scores([3])
    hg_states()
    sd_states()
    hg_outputs()
    sd_outputs()


_SCAN_SCRATCH = [
    pltpu.VMEM((HG_HEADS, HG_F, HG_F), F32),
    pltpu.VMEM((SSD_WIDTH, SSD_STATE), F32),
    pltpu.VMEM((TILE, D_MODEL), BF16),
    pltpu.VMEM((TILE, D_MODEL), BF16),
    pltpu.VMEM((TILE, D_MODEL), BF16),
    pltpu.VMEM((TILE, HG_HEADS // 2 * LANES), BF16),
    pltpu.VMEM((HG_NC, HG_HEADS, HG_F, HG_F), BF16),
    pltpu.VMEM((HG_NC * SUBLANES, D_MODEL), F32),
    pltpu.VMEM((SSD_NC, SSD_CHUNK, LANES), F32),
    pltpu.VMEM((SSD_NC, LANES, SSD_CHUNK), F32),
    pltpu.VMEM((SSD_NC, SSD_GROUPS, SSD_CHUNK, SSD_CHUNK), F32),
    pltpu.VMEM((SSD_NC, SSD_HEADS, SSD_CHUNK, SSD_CHUNK), BF16),
    pltpu.VMEM((TILE, SSD_WIDTH), BF16),
    pltpu.VMEM((TILE, SSD_WIDTH), F32),
    pltpu.VMEM((SSD_NC, SSD_WIDTH, SSD_STATE), BF16),
]


def _fwd_scan_kernel(q_ref, f_ref, v_ref, dtx_ref, bc_ref, dt_ref, lb_ref, a_ref,
                     o_ref, y_ref, shg_s, sssd_s, qd_s, ki_s, ke_s, hsc_s, hsp_s, et_s,
                     cum_s, cumt_s, ssc_s, m_s, dxe_s, ecum_s, ssp_s):
    t = pl.program_id(1)

    @pl.when(t == 0)
    def _():
        shg_s[...] = jnp.zeros_like(shg_s)
        sssd_s[...] = jnp.zeros_like(sssd_s)

    def emit_hg(rows, p, o):
        o_ref[rows, p * HG_PAIR:(p + 1) * HG_PAIR] = o.astype(BF16)

    def emit_ssd(rows, g, ys):
        for i, y in enumerate(ys):
            y_ref[rows, g * SSD_GRP_W + i * LANES:g * SSD_GRP_W + (i + 1) * LANES] = y.astype(BF16)

    hg = _hgrn_stages(q_ref, f_ref, v_ref, lb_ref, shg_s, qd_s, ki_s, ke_s, hsc_s, hsp_s, et_s, False, emit_hg)
    sd = _ssd_stages(dtx_ref, bc_ref, dt_ref, a_ref, sssd_s, cum_s, cumt_s, ssc_s, m_s, dxe_s, ecum_s, ssp_s,
                     False, 0, emit_ssd)
    _run_scan_stages(hg, sd)


def _fwd_scan(q, f, v, dtx, bc, dt, lb, a_pad):
    tok = lambda b, t: (b, t, 0)
    lat = lambda b, t: (b, jnp.maximum(t - 1, 0), 0)
    return pl.pallas_call(
        _fwd_scan_kernel,
        grid=(BATCH, N_TILES),
        in_specs=[
            pl.BlockSpec((None, TILE, D_MODEL), tok),
            pl.BlockSpec((None, TILE, D_MODEL), tok),
            pl.BlockSpec((None, TILE, D_MODEL), tok),
            pl.BlockSpec((None, TILE, SSD_WIDTH), tok),
            pl.BlockSpec((None, TILE, 2 * SSD_BC), tok),
            pl.BlockSpec((None, TILE, LANES), tok),
            pl.BlockSpec((None, 1, D_MODEL), lambda b, t: (0, 0, 0)),
            _const_spec((1, LANES)),
        ],
        out_specs=[pl.BlockSpec((None, TILE, D_MODEL), lat), pl.BlockSpec((None, TILE, SSD_WIDTH), lat)],
        out_shape=[jax.ShapeDtypeStruct((BATCH, SEQ, D_MODEL), BF16),
                   jax.ShapeDtypeStruct((BATCH, SEQ, SSD_WIDTH), BF16)],
        scratch_shapes=_SCAN_SCRATCH,
        compiler_params=pltpu.CompilerParams(
            dimension_semantics=("parallel", "arbitrary"), vmem_limit_bytes=VMEM_LIMIT),
        name="fwd_scan",
    )(q, f, v, dtx, bc, dt, lb, a_pad)


def _bwd_scan_kernel(q_ref, f_ref, v_ref, dtx_ref, xs_ref, bc_ref, dt_ref, g_ref, z_ref, of_ref, yf_ref,
                     lb_ref, a_ref, hnw_ref, dsk_ref, snw_ref, out_ref,
                     shg_s, sssd_s, qd_s, ki_s, ke_s, hsc_s, hsp_s, et_s,
                     cum_s, cumt_s, ssc_s, m_s, dxe_s, ecum_s, ssp_s):
    s = pl.program_id(1)

    @pl.when(s == 0)
    def _():
        shg_s[...] = jnp.zeros_like(shg_s)
        sssd_s[...] = jnp.zeros_like(sssd_s)

    def emit_hg(rows, p, o_b):
        for i in range(2):
            cols = slice(p * HG_PAIR + i * HG_F, p * HG_PAIR + (i + 1) * HG_F)
            o = of_ref[rows, cols] + o_b[:, i * HG_F:(i + 1) * HG_F]
            out_ref[rows, cols] = (_rms(o, hnw_ref[...]) * g_ref[rows, cols]).astype(BF16)

    def emit_ssd(rows, g, ys):
        us = []
        for i, y_b in enumerate(ys):
            cols = slice(g * SSD_GRP_W + i * LANES, g * SSD_GRP_W + (i + 1) * LANES)
            y = yf_ref[rows, cols] + y_b + dsk_ref[:, cols] * xs_ref[rows, cols]
            us.append(y * z_ref[rows, cols])
        ms = sum(jnp.sum(u * u, axis=-1, keepdims=True) for u in us) / SSD_GRP_W
        inv = lax.rsqrt(ms + EPS)
        for i, u in enumerate(us):
            cols = slice(g * SSD_GRP_W + i * LANES, g * SSD_GRP_W + (i + 1) * LANES)
            out_ref[rows, SSD_WIDTH + cols.start:SSD_WIDTH + cols.stop] = (
                u * inv * snw_ref[:, cols]).astype(BF16)

    hg = _hgrn_stages(q_ref, f_ref, v_ref, lb_ref, shg_s, qd_s, ki_s, ke_s, hsc_s, hsp_s, et_s, True, emit_hg)
    sd = _ssd_stages(dtx_ref, bc_ref, dt_ref, a_ref, sssd_s, cum_s, cumt_s, ssc_s, m_s, dxe_s, ecum_s, ssp_s,
                     True, SSD_HEADS, emit_ssd)
    _run_scan_stages(hg, sd)


def _bwd_scan(q, f, v, dtx, xs, bc, dt, g, z, o_f, y_f, lb, a_pad, hg_norm, d_skip, ssd_norm):
    tok = lambda b, s: (b, jnp.where(s == 0, 0, N_TILES - s), 0)
    fcol = lambda b, s: (b, jnp.where(s == 0, 0, N_TILES - s), 1)
    lat = lambda b, s: (b, jnp.where(s == 0, N_LAT_TILES - 1, N_LAT_TILES - s), 0)
    return pl.pallas_call(
        _bwd_scan_kernel,
        grid=(BATCH, N_TILES),
        in_specs=[
            pl.BlockSpec((None, TILE, D_MODEL), tok),
            pl.BlockSpec((None, TILE, D_MODEL), fcol),
            pl.BlockSpec((None, TILE, D_MODEL), tok),
            pl.BlockSpec((None, TILE, SSD_WIDTH), tok),
            pl.BlockSpec((None, TILE, SSD_WIDTH), tok),
            pl.BlockSpec((None, TILE, 2 * SSD_BC), tok),
            pl.BlockSpec((None, TILE, LANES), tok),
            pl.BlockSpec((None, TILE, D_MODEL), tok),
            pl.BlockSpec((None, TILE, SSD_WIDTH), tok),
            pl.BlockSpec((None, TILE, D_MODEL), lat),
            pl.BlockSpec((None, TILE, SSD_WIDTH), lat),
            pl.BlockSpec((None, 1, D_MODEL), lambda b, s: (1, 0, 0)),
            _const_spec((1, LANES)),
            _const_spec((1, HG_F)),
            _const_spec((1, SSD_WIDTH)),
            _const_spec((1, SSD_WIDTH)),
        ],
        out_specs=pl.BlockSpec((None, TILE, 2 * D_MODEL), lat),
        out_shape=jax.ShapeDtypeStruct((BATCH, SEQ, 2 * D_MODEL), BF16),
        scratch_shapes=_SCAN_SCRATCH,
        compiler_params=pltpu.CompilerParams(
            dimension_semantics=("parallel", "arbitrary"), vmem_limit_bytes=VMEM_LIMIT),
        name="bwd_scan",
    )(q, f, v, dtx, xs, bc, dt, g, z, o_f, y_f, lb, a_pad, hg_norm, d_skip, ssd_norm)


def _out_ffn_kernel(y_ref, x_ref, mod_ref, wo_ref, nf_ref, wg_ref, wu_ref, wd_ref, fn_ref,
                    out_ref, u_s, h_s, acc_s):
    g1 = mod_ref[:, 2 * D_MODEL:3 * D_MODEL]
    sh2 = mod_ref[:, 3 * D_MODEL:4 * D_MODEL]
    sc2 = mod_ref[:, 4 * D_MODEL:5 * D_MODEL]
    g2 = mod_ref[:, 5 * D_MODEL:6 * D_MODEL]
    halves = [slice(i * TILE, (i + 1) * TILE) for i in range(FFN_TILE // TILE)]
    n_chunks = D_FF // FFN_CHUNK

    for r in halves:
        h = x_ref[r, :] + g1 * _dot(y_ref[r, :], wo_ref[...])
        h_s[r, :] = h
        u_s[r, :] = (_rms(h, nf_ref[...]) * (1.0 + sc2) + sh2).astype(BF16)

    for r in halves:
        def gate_up(c):
            cols = slice(c * FFN_CHUNK, (c + 1) * FFN_CHUNK)
            return _dot(u_s[r, :], wg_ref[:, cols]), _dot(u_s[r, :], wu_ref[:, cols])

        cur = gate_up(0)
        for c in range(n_chunks):
            nxt = gate_up(c + 1) if c + 1 < n_chunks else None
            part = _dot((_silu(cur[0]) * cur[1]).astype(BF16), wd_ref[c * FFN_CHUNK:(c + 1) * FFN_CHUNK, :])
            if c == 0:
                acc_s[r, :] = part
            else:
                acc_s[r, :] += part
            cur = nxt
        out_ref[r, :] = _rms(h_s[r, :] + g2 * acc_s[r, :], fn_ref[...])


def _out_ffn(ymix, x, mod3, w_out, norm_ffn, w_gate, w_up, w_down, final_norm):
    tok = lambda b, t: (b, t, 0)
    return pl.pallas_call(
        _out_ffn_kernel,
        grid=(BATCH, SEQ // FFN_TILE),
        in_specs=[
            pl.BlockSpec((None, FFN_TILE, 2 * D_MODEL), tok),
            pl.BlockSpec((None, FFN_TILE, D_MODEL), tok),
            pl.BlockSpec((None, 1, N_MOD * D_MODEL), lambda b, t: (b, 0, 0)),
            _const_spec((2 * D_MODEL, D_MODEL)),
            _const_spec((1, D_MODEL)),
            _const_spec((D_MODEL, D_FF)),
            _const_spec((D_MODEL, D_FF)),
            _const_spec((D_FF, D_MODEL)),
            _const_spec((1, D_MODEL)),
        ],
        out_specs=pl.BlockSpec((None, FFN_TILE, D_MODEL), tok),
        out_shape=jax.ShapeDtypeStruct((BATCH, SEQ, D_MODEL), F32),
        scratch_shapes=[pltpu.VMEM((FFN_TILE, D_MODEL), BF16), pltpu.VMEM((FFN_TILE, D_MODEL), F32),
                        pltpu.VMEM((FFN_TILE, D_MODEL), F32)],
        compiler_params=pltpu.CompilerParams(
            dimension_semantics=("parallel", "arbitrary"), vmem_limit_bytes=VMEM_LIMIT),
        name="out_ffn",
    )(ymix, x, mod3, w_out, norm_ffn, w_gate, w_up, w_down, final_norm)


def kernel(x, c, ctx, c_ctx, w_ada, b_ada, norm_mix, w_in, conv_w, conv_b, ssd_a_log, ssd_dt_bias,
           ssd_d, ssd_norm, hgrn_lb_raw, hgrn_norm, w_out, norm_ffn, w_gate, w_up, w_down, final_norm):
    assert x.shape == (BATCH, SEQ, D_MODEL) and ctx.shape == (BATCH, CTX_LEN, D_MODEL)
    assert w_ada.shape == (1, D_MODEL, N_MOD * D_MODEL) and w_in.shape == (1, D_MODEL, D_IN)

    c_all = jnp.concatenate([c, c_ctx[None, :], jnp.zeros((MOD_ROWS - BATCH - 1, D_MODEL), F32)], axis=0)
    w_in_b = w_in[0].astype(BF16)
    w_dt = jnp.pad(w_in_b[:, COL_DT:], ((0, 0), (0, LANES - 2 * SSD_HEADS)))
    dt_bias = jnp.pad(ssd_dt_bias[0].reshape(1, 2 * SSD_HEADS), ((0, 0), (0, LANES - 2 * SSD_HEADS)))
    a_pad = jnp.pad(-jnp.exp(ssd_a_log[0].astype(F32)).reshape(1, 2 * SSD_HEADS),
                    ((0, 0), (0, LANES - 2 * SSD_HEADS)))
    lbs = jnp.cumsum(jax.nn.softmax(hgrn_lb_raw.astype(F32), axis=0), axis=0)[0][:, None, :]
    d_skip = jnp.repeat(ssd_d[0], SSD_HEADDIM)[None, :]

    mod = _ada(c_all, w_ada, b_ada)
    mod3 = mod.reshape(MOD_ROWS, 1, N_MOD * D_MODEL)
    q, f, v, g, z, xs, dtx_f, dtx_b, bc, dt = _in_proj(
        x, ctx, mod3, norm_mix, w_in_b, w_dt, conv_w[0], conv_b, dt_bias)
    o_f, y_f = _fwd_scan(q, f, v, dtx_f, bc, dt, lbs, a_pad)
    ymix = _bwd_scan(q, f, v, dtx_b, xs, bc, dt, g, z, o_f, y_f, lbs, a_pad,
                     hgrn_norm, d_skip, ssd_norm)
    return _out_ffn(ymix, x, mod3, w_out[0].astype(BF16), norm_ffn, w_gate[0].astype(BF16),
                    w_up[0].astype(BF16), w_down[0].astype(BF16), final_norm[None, :])
```

```python
import jax
import jax.numpy as jnp
from jax import lax
from jax.experimental import pallas as pl
from jax.experimental.pallas import tpu as pltpu

F32 = jnp.float32
BF16 = jnp.bfloat16

D_MODEL = 1024
BATCH = 8
SEQ = 4096
CTX_LEN = 256
GRID_W = 64
HG_HEADS = 8
HG_F = 128
HG_CHUNK = 64
SSD_HEADS = 16
SSD_HEADDIM = 64
SSD_GROUPS = 4
SSD_HPG = SSD_HEADS // SSD_GROUPS
SSD_STATE = 128
SSD_CONV = 5
SSD_CHUNK = 128
SSD_WIDTH = SSD_HEADS * SSD_HEADDIM
SSD_BC = SSD_GROUPS * SSD_STATE
D_FF = 2816
N_MOD = 6
EPS = 1e-6
LOG2E = 1.4426950408889634

LANES = 128
SUBLANES = 8

TILE = 256
N_LAT_TILES = SEQ // TILE
N_TILES = N_LAT_TILES + 1
MOD_ROWS = 16
CTX_ROW = BATCH
HG_NC = TILE // HG_CHUNK
SSD_NC = TILE // SSD_CHUNK
HG_PAIR = 2 * HG_F
SSD_GRP_W = SSD_HPG * SSD_HEADDIM

COL_Q = 0
COL_F = 1024
COL_I = 3072
COL_G = 4096
COL_Z = 5120
COL_XBC = 6144
COL_DT = 8192
D_IN = 8224
PROJ_CHUNK = 512
CONV_HALO = SUBLANES
FFN_CHUNK = 256
FFN_TILE = 2 * TILE

VMEM_LIMIT = 56 * 1024 * 1024


def _silu(x):
    h = 0.5 * x
    return h + h * jnp.tanh(h)


def _rms(x, w):
    return x * lax.rsqrt(jnp.mean(x * x, axis=-1, keepdims=True) + EPS) * w


def _dot(a, b):
    return jnp.dot(a, b, preferred_element_type=F32)


def _dot_nt(a, b):
    return lax.dot_general(a, b, (((1,), (1,)), ((), ())), preferred_element_type=F32)


def _dot_tn(a, b):
    return lax.dot_general(a, b, (((0,), (0,)), ((), ())), preferred_element_type=F32)


def _cumsum_mm(tri_b, x):
    hi = x.astype(BF16)
    r1 = x - hi.astype(F32)
    mid = r1.astype(BF16)
    lo = (r1 - mid.astype(F32)).astype(BF16)
    return _dot(tri_b, hi) + _dot(tri_b, mid) + _dot(tri_b, lo)


def _block_diag(a, b):
    z = jnp.zeros_like(a)
    return jnp.concatenate([jnp.concatenate([a, z], axis=1), jnp.concatenate([z, b], axis=1)], axis=0)


def _lane_pair(col_a, col_b):
    n = col_a.shape[0]
    lane = lax.broadcasted_iota(jnp.int32, (n, LANES), 1)
    return jnp.where(lane < SSD_HEADDIM, col_a, col_b)


def _const_spec(shape):
    nd = len(shape)
    return pl.BlockSpec(shape, lambda *_: (0,) * nd, pipeline_mode=pl.Buffered(1))


def _ada_kernel(c_ref, w_ref, b_ref, o_ref):
    act = _silu(c_ref[...]).astype(BF16)
    o_ref[...] = _dot(act, w_ref[...].astype(BF16)) + b_ref[...]


def _ada(c_all, w_ada, b_ada):
    n = N_MOD * D_MODEL
    bn = 1536
    return pl.pallas_call(
        _ada_kernel,
        grid=(n // bn,),
        in_specs=[
            pl.BlockSpec((MOD_ROWS, D_MODEL), lambda j: (0, 0)),
            pl.BlockSpec((None, D_MODEL, bn), lambda j: (0, 0, j)),
            pl.BlockSpec((1, bn), lambda j: (0, j)),
        ],
        out_specs=pl.BlockSpec((MOD_ROWS, bn), lambda j: (0, j)),
        out_shape=jax.ShapeDtypeStruct((MOD_ROWS, n), F32),
        compiler_params=pltpu.CompilerParams(vmem_limit_bytes=VMEM_LIMIT),
        name="ada",
    )(c_all, w_ada, b_ada)


def _in_proj_kernel(x_ref, ctx_ref, mod_ref, nw_ref, w_ref, wdt_ref, cw_ref, cb_ref, dtb_ref,
                    q_ref, f_ref, v_ref, g_ref, z_ref, xs_ref, dxf_ref, dxb_ref, bc_ref, dt_ref,
                    u_s, xbc_s):
    t = pl.program_id(1)
    is_ctx = t == 0
    x = jnp.where(is_ctx, ctx_ref[...], x_ref[...])
    shift = mod_ref[:, 0:D_MODEL]
    scale = mod_ref[:, D_MODEL:2 * D_MODEL]
    u_s[...] = (_rms(x, nw_ref[...] * (1.0 + scale)) + shift).astype(BF16)

    def proj(c0, width):
        return _dot(u_s[...], w_ref[:, c0:c0 + width])

    dt = jax.nn.softplus(_dot(u_s[...], wdt_ref[...]) + dtb_ref[...])
    dt_ref[...] = dt
    zero_halo = jnp.zeros((CONV_HALO, SSD_WIDTH + 2 * SSD_BC), F32)
    xbc_s[0:CONV_HALO, :] = zero_halo
    xbc_s[CONV_HALO + TILE:, :] = zero_halo
    for c0 in range(COL_XBC, COL_DT, PROJ_CHUNK):
        xbc_s[CONV_HALO:CONV_HALO + TILE, c0 - COL_XBC:c0 - COL_XBC + PROJ_CHUNK] = proj(c0, PROJ_CHUNK)

    def proj_chunk(c0):
        y = proj(c0, PROJ_CHUNK)
        if c0 < COL_F:
            q_ref[:, c0 - COL_Q:c0 - COL_Q + PROJ_CHUNK] = (_silu(y) * (HG_F ** -0.5)).astype(BF16)
        elif c0 < COL_I:
            f_ref[:, c0 - COL_F:c0 - COL_F + PROJ_CHUNK] = y
        elif c0 < COL_G:
            v_ref[:, c0 - COL_I:c0 - COL_I + PROJ_CHUNK] = y.astype(BF16)
        elif c0 < COL_Z:
            g_ref[:, c0 - COL_G:c0 - COL_G + PROJ_CHUNK] = _silu(y).astype(BF16)
        else:
            z_ref[:, c0 - COL_Z:c0 - COL_Z + PROJ_CHUNK] = _silu(y).astype(BF16)

    pad = SSD_CONV // 2

    def conv_slab(j):
        cols = slice(j * LANES, (j + 1) * LANES)
        blk_rows = GRID_W + 2 * CONV_HALO
        for r0 in range(0, TILE, GRID_W):
            rows = slice(r0, r0 + GRID_W)
            blk = xbc_s[r0:r0 + blk_rows, cols]
            below = jnp.where(is_ctx, blk[:CONV_HALO], 0.0)
            above = jnp.where(is_ctx, blk[CONV_HALO + GRID_W:], 0.0)
            blk = jnp.concatenate([below, blk[CONV_HALO:CONV_HALO + GRID_W], above], axis=0)
            acc = jnp.broadcast_to(cb_ref[:, cols], (GRID_W, LANES))
            for k in range(SSD_CONV):
                win = blk if k == pad else pltpu.roll(blk, (pad - k) % blk_rows, axis=0)
                acc = acc + cw_ref[k:k + 1, cols] * win[CONV_HALO:CONV_HALO + GRID_W]
            act = _silu(acc)
            if j < SSD_WIDTH // LANES:
                xs_ref[rows, cols] = act.astype(BF16)
                for dref, h0 in ((dxf_ref, 2 * j), (dxb_ref, SSD_HEADS + 2 * j)):
                    dref[rows, cols] = (_lane_pair(dt[rows, h0:h0 + 1], dt[rows, h0 + 1:h0 + 2]) * act).astype(BF16)
            else:
                bc_ref[rows, j * LANES - SSD_WIDTH:(j + 1) * LANES - SSD_WIDTH] = act.astype(BF16)

    for j in range((SSD_WIDTH + 2 * SSD_BC) // LANES):
        conv_slab(j)
    for c0 in (list(range(COL_F, COL_G, PROJ_CHUNK)) + list(range(COL_Q, COL_F, PROJ_CHUNK))
               + list(range(COL_G, COL_XBC, PROJ_CHUNK))):
        proj_chunk(c0)


def _in_proj(x, ctx, mod3, norm_w, w_in, w_dt, conv_w, conv_b, dt_bias):
    lat = lambda b, t: (b, jnp.maximum(t - 1, 0), 0)
    tok = lambda b, t: (b, t, 0)
    n_tok = N_TILES * TILE

    def out(width, dtype):
        return jax.ShapeDtypeStruct((BATCH, n_tok, width), dtype), pl.BlockSpec((None, TILE, width), tok)

    outs = [out(D_MODEL, BF16), out(2 * D_MODEL, F32), out(D_MODEL, BF16), out(D_MODEL, BF16),
            out(D_MODEL, BF16), out(SSD_WIDTH, BF16), out(SSD_WIDTH, BF16), out(SSD_WIDTH, BF16),
            out(2 * SSD_BC, BF16), out(LANES, F32)]
    return pl.pallas_call(
        _in_proj_kernel,
        grid=(BATCH, N_TILES),
        in_specs=[
            pl.BlockSpec((None, TILE, D_MODEL), lat),
            pl.BlockSpec((None, CTX_LEN, D_MODEL), lambda b, t: (b, 0, 0)),
            pl.BlockSpec((None, 1, N_MOD * D_MODEL), lambda b, t: (jnp.where(t == 0, CTX_ROW, b), 0, 0)),
            _const_spec((1, D_MODEL)),
            _const_spec((D_MODEL, D_IN)),
            _const_spec((D_MODEL, LANES)),
            _const_spec((SSD_CONV, SSD_WIDTH + 2 * SSD_BC)),
            _const_spec((1, SSD_WIDTH + 2 * SSD_BC)),
            _const_spec((1, LANES)),
        ],
        out_specs=[o[1] for o in outs],
        out_shape=[o[0] for o in outs],
        scratch_shapes=[
            pltpu.VMEM((TILE, D_MODEL), BF16),
            pltpu.VMEM((TILE + 2 * CONV_HALO, SSD_WIDTH + 2 * SSD_BC), F32),
        ],
        compiler_params=pltpu.CompilerParams(
            dimension_semantics=("parallel", "arbitrary"), vmem_limit_bytes=VMEM_LIMIT),
        name="in_proj",
    )(x, ctx, mod3, norm_w, w_in, w_dt, conv_w, conv_b, dt_bias)


def _tri(n, reverse):
    i = lax.broadcasted_iota(jnp.int32, (n, n), 0)
    j = lax.broadcasted_iota(jnp.int32, (n, n), 1)
    return (j >= i) if reverse else (j <= i)


def _hgrn_stages(q_ref, f_ref, v_ref, lb_ref, s_ref, qd_s, ki_s, ke_s, sc_s, sp_s, et_s, reverse, emit):
    mask = _tri(HG_CHUNK, reverse)
    tri_b = jnp.where(mask, 1.0, 0.0).astype(BF16)
    i2 = lax.broadcasted_iota(jnp.int32, (HG_CHUNK, LANES), 0)
    j2 = lax.broadcasted_iota(jnp.int32, (HG_CHUNK, LANES), 1) & (HG_CHUNK - 1)
    mask2 = (j2 >= i2) if reverse else (j2 <= i2)
    last = slice(0, 1) if reverse else slice(HG_CHUNK - 1, HG_CHUNK)
    chunk_rows = [slice(c * HG_CHUNK, (c + 1) * HG_CHUNK) for c in range(HG_NC)]
    pair_cols = [slice(p * HG_PAIR, (p + 1) * HG_PAIR) for p in range(HG_HEADS // 2)]
    order = list(reversed(range(HG_NC))) if reverse else list(range(HG_NC))

    def gates(chunks=range(HG_NC)):
        for c in chunks:
            rows = chunk_rows[c]
            for cols in pair_cols:
                lb = lb_ref[:, cols]
                f = lb + (1.0 - lb) * jax.nn.sigmoid(f_ref[rows, cols])
                k = 1.0 - f
                cum = _cumsum_mm(tri_b, jnp.log(f)) * LOG2E
                e_cum = jnp.exp2(cum)
                e_tot = e_cum[last, :]
                k_inv = k / e_cum
                qd_s[rows, cols] = (q_ref[rows, cols] * e_cum).astype(BF16)
                ki_s[rows, cols] = k_inv.astype(BF16)
                ke_s[rows, cols] = (k_inv * e_tot).astype(BF16)
                et_s[c * SUBLANES:c * SUBLANES + 1, cols] = e_tot

    def scores(chunks=range(HG_NC)):
        for c in chunks:
            rows = chunk_rows[c]
            for p, cols in enumerate(pair_cols):
                kp = ki_s[rows, cols]
                sc = _dot_nt(qd_s[rows, cols], _block_diag(kp[:, :HG_F], kp[:, HG_F:]))
                sc_s[rows, p * LANES:(p + 1) * LANES] = jnp.where(mask2, sc, 0.0).astype(BF16)

    def states():
        for h in range(HG_HEADS):
            cols = slice(h * HG_F, (h + 1) * HG_F)
            s = s_ref[h]
            for c in order:
                rows = chunk_rows[c]
                sp_s[c, h] = s.astype(BF16)
                s = s * et_s[c * SUBLANES:c * SUBLANES + 1, cols] + _dot_tn(v_ref[rows, cols], ke_s[rows, cols])
            s_ref[h] = s

    def outputs():
        for c in range(HG_NC):
            rows = chunk_rows[c]
            for p, cols in enumerate(pair_cols):
                vp = v_ref[rows, cols]
                o = _dot(sc_s[rows, p * LANES:(p + 1) * LANES], _block_diag(vp[:, :HG_F], vp[:, HG_F:]))
                o = o + _dot_nt(qd_s[rows, cols], _block_diag(sp_s[c, 2 * p], sp_s[c, 2 * p + 1]))
                emit(rows, p, o)

    return [gates, scores, states, outputs]


def _ssd_stages(dtx_ref, bc_ref, dt_ref, a_ref, s_ref, cum_s, cumt_s, sc_s, m_s, dxe_s, ecum_s, sp_s,
                reverse, dcol, emit):
    mask = _tri(SSD_CHUNK, reverse)
    tri_b = jnp.where(mask, 1.0, 0.0).astype(BF16)
    last = slice(0, 1) if reverse else slice(SSD_CHUNK - 1, SSD_CHUNK)
    chunk_rows = [slice(c * SSD_CHUNK, (c + 1) * SSD_CHUNK) for c in range(SSD_NC)]
    order = list(reversed(range(SSD_NC))) if reverse else list(range(SSD_NC))
    lane = lax.broadcasted_iota(jnp.int32, (SSD_CHUNK, LANES), 1)
    low = lane < SSD_HEADDIM

    def decays():
        for c in range(SSD_NC):
            cum = _cumsum_mm(tri_b, dt_ref[chunk_rows[c], :] * (a_ref[...] * LOG2E))
            cum_s[c] = cum
            cumt_s[c] = cum.T

    def scores():
        for c in range(SSD_NC):
            rows = chunk_rows[c]
            for g in range(SSD_GROUPS):
                bg = bc_ref[rows, g * SSD_STATE:(g + 1) * SSD_STATE]
                cg = bc_ref[rows, SSD_BC + g * SSD_STATE:SSD_BC + (g + 1) * SSD_STATE]
                sc_s[c, g] = _dot_nt(cg, bg)

    def masks(chunks=range(SSD_NC)):
        for c in chunks:
            rows = chunk_rows[c]
            cum = cum_s[c]
            cum_t = cumt_s[c]
            for pair in range(SSD_HEADS // 2):
                rs, tots = [], []
                for h in (2 * pair, 2 * pair + 1):
                    col = dcol + h
                    r = jnp.broadcast_to(cum[:, col:col + 1], (SSD_CHUNK, SSD_CHUNK))
                    row_t = cum_t[col:col + 1, :]
                    decay = jnp.exp2(jnp.where(mask, r - row_t, -jnp.inf))
                    m_s[c, h] = (sc_s[c, h // SSD_HPG] * decay).astype(BF16)
                    rs.append(r)
                    tots.append(row_t[:, last.start:last.stop])
                r2 = jnp.where(low, rs[0], rs[1])
                tot2 = jnp.where(low[0:1, :], tots[0], tots[1])
                cols = slice(pair * LANES, (pair + 1) * LANES)
                ecum_s[rows, cols] = jnp.exp2(r2)
                dxe_s[rows, cols] = (jnp.exp2(tot2 - r2) * dtx_ref[rows, cols].astype(F32)).astype(BF16)

    def states():
        for g in range(SSD_GROUPS):
            grp = slice(g * SSD_GRP_W, (g + 1) * SSD_GRP_W)
            s = s_ref[grp, :]
            for c in order:
                rows = chunk_rows[c]
                sp_s[c, grp, :] = s.astype(BF16)
                ds = _dot_tn(dxe_s[rows, grp], bc_ref[rows, g * SSD_STATE:(g + 1) * SSD_STATE])
                tot_t = cumt_s[c][:, last.start:last.stop]
                dec = jnp.concatenate(
                    [jnp.broadcast_to(jnp.exp2(tot_t[dcol + g * SSD_HPG + hh:dcol + g * SSD_HPG + hh + 1, :]),
                                      (SSD_HEADDIM, SSD_STATE)) for hh in range(SSD_HPG)], axis=0)
                s = s * dec + ds
            s_ref[grp, :] = s

    def outputs():
        for c in range(SSD_NC):
            rows = chunk_rows[c]
            for g in range(SSD_GROUPS):
                grp = slice(g * SSD_GRP_W, (g + 1) * SSD_GRP_W)
                cg = bc_ref[rows, SSD_BC + g * SSD_STATE:SSD_BC + (g + 1) * SSD_STATE]
                y_inter = _dot_nt(cg, sp_s[c, grp, :])
                ys = []
                for pp in range(SSD_HPG // 2):
                    pair = g * (SSD_HPG // 2) + pp
                    cols = slice(pair * LANES, (pair + 1) * LANES)
                    dp = dtx_ref[rows, cols]
                    zero = jnp.zeros_like(dp)
                    rhs = jnp.concatenate([jnp.where(low, dp, zero), jnp.where(low, zero, dp)], axis=0)
                    y_intra = _dot(jnp.concatenate([m_s[c, 2 * pair], m_s[c, 2 * pair + 1]], axis=1), rhs)
                    ys.append(y_intra + ecum_s[rows, cols] * y_inter[:, pp * LANES:(pp + 1) * LANES])
                emit(rows, g, ys)

    return [decays, scores, masks, states, outputs]


def _run_scan_stages(hg, sd):
    gates, hg_scores, hg_states, hg_outputs = hg
    decays, sd_scores, masks, sd_states, sd_outputs = sd
    decays()
    gates([0])
    sd_scores()
    hg_scores([0])
    gates([1])
    masks([0])
    hg_scores([1])
    gates([2])
    masks([1])
    hg_scores([2])
    gates([3])
    hg_scores([3])
    hg_states()
    sd_states()
    hg_outputs()
    sd_outputs()


_SCAN_SCRATCH = [
    pltpu.VMEM((HG_HEADS, HG_F, HG_F), F32),
    pltpu.VMEM((SSD_WIDTH, SSD_STATE), F32),
    pltpu.VMEM((TILE, D_MODEL), BF16),
    pltpu.VMEM((TILE, D_MODEL), BF16),
    pltpu.VMEM((TILE, D_MODEL), BF16),
    pltpu.VMEM((TILE, HG_HEADS // 2 * LANES), BF16),
    pltpu.VMEM((HG_NC, HG_HEADS, HG_F, HG_F), BF16),
    pltpu.VMEM((HG_NC * SUBLANES, D_MODEL), F32),
    pltpu.VMEM((SSD_NC, SSD_CHUNK, LANES), F32),
    pltpu.VMEM((SSD_NC, LANES, SSD_CHUNK), F32),
    pltpu.VMEM((SSD_NC, SSD_GROUPS, SSD_CHUNK, SSD_CHUNK), F32),
    pltpu.VMEM((SSD_NC, SSD_HEADS, SSD_CHUNK, SSD_CHUNK), BF16),
    pltpu.VMEM((TILE, SSD_WIDTH), BF16),
    pltpu.VMEM((TILE, SSD_WIDTH), F32),
    pltpu.VMEM((SSD_NC, SSD_WIDTH, SSD_STATE), BF16),
]


def _fwd_scan_kernel(q_ref, f_ref, v_ref, dtx_ref, bc_ref, dt_ref, lb_ref, a_ref,
                     o_ref, y_ref, shg_s, sssd_s, qd_s, ki_s, ke_s, hsc_s, hsp_s, et_s,
                     cum_s, cumt_s, ssc_s, m_s, dxe_s, ecum_s, ssp_s):
    t = pl.program_id(1)

    @pl.when(t == 0)
    def _():
        shg_s[...] = jnp.zeros_like(shg_s)
        sssd_s[...] = jnp.zeros_like(sssd_s)

    def emit_hg(rows, p, o):
        o_ref[rows, p * HG_PAIR:(p + 1) * HG_PAIR] = o.astype(BF16)

    def emit_ssd(rows, g, ys):
        for i, y in enumerate(ys):
            y_ref[rows, g * SSD_GRP_W + i * LANES:g * SSD_GRP_W + (i + 1) * LANES] = y.astype(BF16)

    hg = _hgrn_stages(q_ref, f_ref, v_ref, lb_ref, shg_s, qd_s, ki_s, ke_s, hsc_s, hsp_s, et_s, False, emit_hg)
    sd = _ssd_stages(dtx_ref, bc_ref, dt_ref, a_ref, sssd_s, cum_s, cumt_s, ssc_s, m_s, dxe_s, ecum_s, ssp_s,
                     False, 0, emit_ssd)
    _run_scan_stages(hg, sd)


def _fwd_scan(q, f, v, dtx, bc, dt, lb, a_pad):
    tok = lambda b, t: (b, t, 0)
    lat = lambda b, t: (b, jnp.maximum(t - 1, 0), 0)
    return pl.pallas_call(
        _fwd_scan_kernel,
        grid=(BATCH, N_TILES),
        in_specs=[
            pl.BlockSpec((None, TILE, D_MODEL), tok),
            pl.BlockSpec((None, TILE, D_MODEL), tok),
            pl.BlockSpec((None, TILE, D_MODEL), tok),
            pl.BlockSpec((None, TILE, SSD_WIDTH), tok),
            pl.BlockSpec((None, TILE, 2 * SSD_BC), tok),
            pl.BlockSpec((None, TILE, LANES), tok),
            pl.BlockSpec((None, 1, D_MODEL), lambda b, t: (0, 0, 0)),
            _const_spec((1, LANES)),
        ],
        out_specs=[pl.BlockSpec((None, TILE, D_MODEL), lat), pl.BlockSpec((None, TILE, SSD_WIDTH), lat)],
        out_shape=[jax.ShapeDtypeStruct((BATCH, SEQ, D_MODEL), BF16),
                   jax.ShapeDtypeStruct((BATCH, SEQ, SSD_WIDTH), BF16)],
        scratch_shapes=_SCAN_SCRATCH,
        compiler_params=pltpu.CompilerParams(
            dimension_semantics=("parallel", "arbitrary"), vmem_limit_bytes=VMEM_LIMIT),
        name="fwd_scan",
    )(q, f, v, dtx, bc, dt, lb, a_pad)


def _bwd_scan_kernel(q_ref, f_ref, v_ref, dtx_ref, xs_ref, bc_ref, dt_ref, g_ref, z_ref, of_ref, yf_ref,
                     lb_ref, a_ref, hnw_ref, dsk_ref, snw_ref, out_ref,
                     shg_s, sssd_s, qd_s, ki_s, ke_s, hsc_s, hsp_s, et_s,
                     cum_s, cumt_s, ssc_s, m_s, dxe_s, ecum_s, ssp_s):
    s = pl.program_id(1)

    @pl.when(s == 0)
    def _():
        shg_s[...] = jnp.zeros_like(shg_s)
        sssd_s[...] = jnp.zeros_like(sssd_s)

    def emit_hg(rows, p, o_b):
        for i in range(2):
            cols = slice(p * HG_PAIR + i * HG_F, p * HG_PAIR + (i + 1) * HG_F)
            o = of_ref[rows, cols] + o_b[:, i * HG_F:(i + 1) * HG_F]
            out_ref[rows, cols] = (_rms(o, hnw_ref[...]) * g_ref[rows, cols]).astype(BF16)

    def emit_ssd(rows, g, ys):
        us = []
        for i, y_b in enumerate(ys):
            cols = slice(g * SSD_GRP_W + i * LANES, g * SSD_GRP_W + (i + 1) * LANES)
            y = yf_ref[rows, cols] + y_b + dsk_ref[:, cols] * xs_ref[rows, cols]
            us.append(y * z_ref[rows, cols])
        ms = sum(jnp.sum(u * u, axis=-1, keepdims=True) for u in us) / SSD_GRP_W
        inv = lax.rsqrt(ms + EPS)
        for i, u in enumerate(us):
            cols = slice(g * SSD_GRP_W + i * LANES, g * SSD_GRP_W + (i + 1) * LANES)
            out_ref[rows, SSD_WIDTH + cols.start:SSD_WIDTH + cols.stop] = (
                u * inv * snw_ref[:, cols]).astype(BF16)

    hg = _hgrn_stages(q_ref, f_ref, v_ref, lb_ref, shg_s, qd_s, ki_s, ke_s, hsc_s, hsp_s, et_s, True, emit_hg)
    sd = _ssd_stages(dtx_ref, bc_ref, dt_ref, a_ref, sssd_s, cum_s, cumt_s, ssc_s, m_s, dxe_s, ecum_s, ssp_s,
                     True, SSD_HEADS, emit_ssd)
    _run_scan_stages(hg, sd)


def _bwd_scan(q, f, v, dtx, xs, bc, dt, g, z, o_f, y_f, lb, a_pad, hg_norm, d_skip, ssd_norm):
    tok = lambda b, s: (b, jnp.where(s == 0, 0, N_TILES - s), 0)
    fcol = lambda b, s: (b, jnp.where(s == 0, 0, N_TILES - s), 1)
    lat = lambda b, s: (b, jnp.where(s == 0, N_LAT_TILES - 1, N_LAT_TILES - s), 0)
    return pl.pallas_call(
        _bwd_scan_kernel,
        grid=(BATCH, N_TILES),
        in_specs=[
            pl.BlockSpec((None, TILE, D_MODEL), tok),
            pl.BlockSpec((None, TILE, D_MODEL), fcol),
            pl.BlockSpec((None, TILE, D_MODEL), tok),
            pl.BlockSpec((None, TILE, SSD_WIDTH), tok),
            pl.BlockSpec((None, TILE, SSD_WIDTH), tok),
            pl.BlockSpec((None, TILE, 2 * SSD_BC), tok),
            pl.BlockSpec((None, TILE, LANES), tok),
            pl.BlockSpec((None, TILE, D_MODEL), tok),
            pl.BlockSpec((None, TILE, SSD_WIDTH), tok),
            pl.BlockSpec((None, TILE, D_MODEL), lat),
            pl.BlockSpec((None, TILE, SSD_WIDTH), lat),
            pl.BlockSpec((None, 1, D_MODEL), lambda b, s: (1, 0, 0)),
            _const_spec((1, LANES)),
            _const_spec((1, HG_F)),
            _const_spec((1, SSD_WIDTH)),
            _const_spec((1, SSD_WIDTH)),
        ],
        out_specs=pl.BlockSpec((None, TILE, 2 * D_MODEL), lat),
        out_shape=jax.ShapeDtypeStruct((BATCH, SEQ, 2 * D_MODEL), BF16),
        scratch_shapes=_SCAN_SCRATCH,
        compiler_params=pltpu.CompilerParams(
            dimension_semantics=("parallel", "arbitrary"), vmem_limit_bytes=VMEM_LIMIT),
        name="bwd_scan",
    )(q, f, v, dtx, xs, bc, dt, g, z, o_f, y_f, lb, a_pad, hg_norm, d_skip, ssd_norm)


def _out_ffn_kernel(y_ref, x_ref, mod_ref, wo_ref, nf_ref, wg_ref, wu_ref, wd_ref, fn_ref,
                    out_ref, u_s, h_s, acc_s):
    g1 = mod_ref[:, 2 * D_MODEL:3 * D_MODEL]
    sh2 = mod_ref[:, 3 * D_MODEL:4 * D_MODEL]
    sc2 = mod_ref[:, 4 * D_MODEL:5 * D_MODEL]
    g2 = mod_ref[:, 5 * D_MODEL:6 * D_MODEL]
    halves = [slice(i * TILE, (i + 1) * TILE) for i in range(FFN_TILE // TILE)]
    n_chunks = D_FF // FFN_CHUNK

    for r in halves:
        h = x_ref[r, :] + g1 * _dot(y_ref[r, :], wo_ref[...])
        h_s[r, :] = h
        u_s[r, :] = (_rms(h, nf_ref[...]) * (1.0 + sc2) + sh2).astype(BF16)

    for r in halves:
        def gate_up(c):
            cols = slice(c * FFN_CHUNK, (c + 1) * FFN_CHUNK)
            return _dot(u_s[r, :], wg_ref[:, cols]), _dot(u_s[r, :], wu_ref[:, cols])

        cur = gate_up(0)
        for c in range(n_chunks):
            nxt = gate_up(c + 1) if c + 1 < n_chunks else None
            part = _dot((_silu(cur[0]) * cur[1]).astype(BF16), wd_ref[c * FFN_CHUNK:(c + 1) * FFN_CHUNK, :])
            if c == 0:
                acc_s[r, :] = part
            else:
                acc_s[r, :] += part
            cur = nxt
        out_ref[r, :] = _rms(h_s[r, :] + g2 * acc_s[r, :], fn_ref[...])


def _out_ffn(ymix, x, mod3, w_out, norm_ffn, w_gate, w_up, w_down, final_norm):
    tok = lambda b, t: (b, t, 0)
    return pl.pallas_call(
        _out_ffn_kernel,
        grid=(BATCH, SEQ // FFN_TILE),
        in_specs=[
            pl.BlockSpec((None, FFN_TILE, 2 * D_MODEL), tok),
            pl.BlockSpec((None, FFN_TILE, D_MODEL), tok),
            pl.BlockSpec((None, 1, N_MOD * D_MODEL), lambda b, t: (b, 0, 0)),
            _const_spec((2 * D_MODEL, D_MODEL)),
            _const_spec((1, D_MODEL)),
            _const_spec((D_MODEL, D_FF)),
            _const_spec((D_MODEL, D_FF)),
            _const_spec((D_FF, D_MODEL)),
            _const_spec((1, D_MODEL)),
        ],
        out_specs=pl.BlockSpec((None, FFN_TILE, D_MODEL), tok),
        out_shape=jax.ShapeDtypeStruct((BATCH, SEQ, D_MODEL), F32),
        scratch_shapes=[pltpu.VMEM((FFN_TILE, D_MODEL), BF16), pltpu.VMEM((FFN_TILE, D_MODEL), F32),
                        pltpu.VMEM((FFN_TILE, D_MODEL), F32)],
        compiler_params=pltpu.CompilerParams(
            dimension_semantics=("parallel", "arbitrary"), vmem_limit_bytes=VMEM_LIMIT),
        name="out_ffn",
    )(ymix, x, mod3, w_out, norm_ffn, w_gate, w_up, w_down, final_norm)


def kernel(x, c, ctx, c_ctx, w_ada, b_ada, norm_mix, w_in, conv_w, conv_b, ssd_a_log, ssd_dt_bias,
           ssd_d, ssd_norm, hgrn_lb_raw, hgrn_norm, w_out, norm_ffn, w_gate, w_up, w_down, final_norm):
    assert x.shape == (BATCH, SEQ, D_MODEL) and ctx.shape == (BATCH, CTX_LEN, D_MODEL)
    assert w_ada.shape == (1, D_MODEL, N_MOD * D_MODEL) and w_in.shape == (1, D_MODEL, D_IN)

    c_all = jnp.concatenate([c, c_ctx[None, :], jnp.zeros((MOD_ROWS - BATCH - 1, D_MODEL), F32)], axis=0)
    w_in_b = w_in[0].astype(BF16)
    w_dt = jnp.pad(w_in_b[:, COL_DT:], ((0, 0), (0, LANES - 2 * SSD_HEADS)))
    dt_bias = jnp.pad(ssd_dt_bias[0].reshape(1, 2 * SSD_HEADS), ((0, 0), (0, LANES - 2 * SSD_HEADS)))
    a_pad = jnp.pad(-jnp.exp(ssd_a_log[0].astype(F32)).reshape(1, 2 * SSD_HEADS),
                    ((0, 0), (0, LANES - 2 * SSD_HEADS)))
    lbs = jnp.cumsum(jax.nn.softmax(hgrn_lb_raw.astype(F32), axis=0), axis=0)[0][:, None, :]
    d_skip = jnp.repeat(ssd_d[0], SSD_HEADDIM)[None, :]

    mod = _ada(c_all, w_ada, b_ada)
    mod3 = mod.reshape(MOD_ROWS, 1, N_MOD * D_MODEL)
    q, f, v, g, z, xs, dtx_f, dtx_b, bc, dt = _in_proj(
        x, ctx, mod3, norm_mix, w_in_b, w_dt, conv_w[0], conv_b, dt_bias)
    o_f, y_f = _fwd_scan(q, f, v, dtx_f, bc, dt, lbs, a_pad)
    ymix = _bwd_scan(q, f, v, dtx_b, xs, bc, dt, g, z, o_f, y_f, lbs, a_pad,
                     hgrn_norm, d_skip, ssd_norm)
    return _out_ffn(ymix, x, mod3, w_out[0].astype(BF16), norm_ffn, w_gate[0].astype(BF16),
                    w_up[0].astype(BF16), w_down[0].astype(BF16), final_norm[None, :])
```

```python
import jax
import jax.numpy as jnp
from jax import lax
from jax.experimental import pallas as pl
from jax.experimental.pallas import tpu as pltpu

F32 = jnp.float32
BF16 = jnp.bfloat16

D_MODEL = 1024
BATCH = 8
SEQ = 4096
CTX_LEN = 256
GRID_W = 64
HG_HEADS = 8
HG_F = 128
HG_CHUNK = 64
SSD_HEADS = 16
SSD_HEADDIM = 64
SSD_GROUPS = 4
SSD_HPG = SSD_HEADS // SSD_GROUPS
SSD_STATE = 128
SSD_CONV = 5
SSD_CHUNK = 128
SSD_WIDTH = SSD_HEADS * SSD_HEADDIM
SSD_BC = SSD_GROUPS * SSD_STATE
D_FF = 2816
N_MOD = 6
EPS = 1e-6
LOG2E = 1.4426950408889634

LANES = 128
SUBLANES = 8

TILE = 256
N_LAT_TILES = SEQ // TILE
N_TILES = N_LAT_TILES + 1
MOD_ROWS = 16
CTX_ROW = BATCH
HG_NC = TILE // HG_CHUNK
SSD_NC = TILE // SSD_CHUNK
HG_PAIR = 2 * HG_F
SSD_GRP_W = SSD_HPG * SSD_HEADDIM

COL_Q = 0
COL_F = 1024
COL_I = 3072
COL_G = 4096
COL_Z = 5120
COL_XBC = 6144
COL_DT = 8192
D_IN = 8224
PROJ_CHUNK = 512
CONV_HALO = SUBLANES
FFN_CHUNK = 256
FFN_TILE = 2 * TILE

VMEM_LIMIT = 56 * 1024 * 1024


def _silu(x):
    h = 0.5 * x
    return h + h * jnp.tanh(h)


def _rms(x, w):
    return x * lax.rsqrt(jnp.mean(x * x, axis=-1, keepdims=True) + EPS) * w


def _dot(a, b):
    return jnp.dot(a, b, preferred_element_type=F32)


def _dot_nt(a, b):
    return lax.dot_general(a, b, (((1,), (1,)), ((), ())), preferred_element_type=F32)


def _dot_tn(a, b):
    return lax.dot_general(a, b, (((0,), (0,)), ((), ())), preferred_element_type=F32)


def _cumsum_mm(tri_b, x):
    hi = x.astype(BF16)
    r1 = x - hi.astype(F32)
    mid = r1.astype(BF16)
    lo = (r1 - mid.astype(F32)).astype(BF16)
    return _dot(tri_b, hi) + _dot(tri_b, mid) + _dot(tri_b, lo)


def _block_diag(a, b):
    z = jnp.zeros_like(a)
    return jnp.concatenate([jnp.concatenate([a, z], axis=1), jnp.concatenate([z, b], axis=1)], axis=0)


def _lane_pair(col_a, col_b):
    n = col_a.shape[0]
    lane = lax.broadcasted_iota(jnp.int32, (n, LANES), 1)
    return jnp.where(lane < SSD_HEADDIM, col_a, col_b)


def _const_spec(shape):
    nd = len(shape)
    return pl.BlockSpec(shape, lambda *_: (0,) * nd, pipeline_mode=pl.Buffered(1))


def _ada_kernel(c_ref, w_ref, b_ref, o_ref):
    act = _silu(c_ref[...]).astype(BF16)
    o_ref[...] = _dot(act, w_ref[...].astype(BF16)) + b_ref[...]


def _ada(c_all, w_ada, b_ada):
    n = N_MOD * D_MODEL
    bn = 1536
    return pl.pallas_call(
        _ada_kernel,
        grid=(n // bn,),
        in_specs=[
            pl.BlockSpec((MOD_ROWS, D_MODEL), lambda j: (0, 0)),
            pl.BlockSpec((None, D_MODEL, bn), lambda j: (0, 0, j)),
            pl.BlockSpec((1, bn), lambda j: (0, j)),
        ],
        out_specs=pl.BlockSpec((MOD_ROWS, bn), lambda j: (0, j)),
        out_shape=jax.ShapeDtypeStruct((MOD_ROWS, n), F32),
        compiler_params=pltpu.CompilerParams(vmem_limit_bytes=VMEM_LIMIT),
        name="ada",
    )(c_all, w_ada, b_ada)


def _in_proj_kernel(x_ref, ctx_ref, mod_ref, nw_ref, w_ref, wdt_ref, cw_ref, cb_ref, dtb_ref,
                    q_ref, f_ref, v_ref, g_ref, z_ref, xs_ref, dxf_ref, dxb_ref, bc_ref, dt_ref,
                    u_s, xbc_s):
    t = pl.program_id(1)
    is_ctx = t == 0
    x = jnp.where(is_ctx, ctx_ref[...], x_ref[...])
    shift = mod_ref[:, 0:D_MODEL]
    scale = mod_ref[:, D_MODEL:2 * D_MODEL]
    u_s[...] = (_rms(x, nw_ref[...] * (1.0 + scale)) + shift).astype(BF16)

    def proj(c0, width):
        return _dot(u_s[...], w_ref[:, c0:c0 + width])

    dt = jax.nn.softplus(_dot(u_s[...], wdt_ref[...]) + dtb_ref[...])
    dt_ref[...] = dt
    zero_halo = jnp.zeros((CONV_HALO, SSD_WIDTH + 2 * SSD_BC), F32)
    xbc_s[0:CONV_HALO, :] = zero_halo
    xbc_s[CONV_HALO + TILE:, :] = zero_halo
    for c0 in range(COL_XBC, COL_DT, PROJ_CHUNK):
        xbc_s[CONV_HALO:CONV_HALO + TILE, c0 - COL_XBC:c0 - COL_XBC + PROJ_CHUNK] = proj(c0, PROJ_CHUNK)

    def proj_chunk(c0):
        y = proj(c0, PROJ_CHUNK)
        if c0 < COL_F:
            q_ref[:, c0 - COL_Q:c0 - COL_Q + PROJ_CHUNK] = (_silu(y) * (HG_F ** -0.5)).astype(BF16)
        elif c0 < COL_I:
            f_ref[:, c0 - COL_F:c0 - COL_F + PROJ_CHUNK] = y
        elif c0 < COL_G:
            v_ref[:, c0 - COL_I:c0 - COL_I + PROJ_CHUNK] = y.astype(BF16)
        elif c0 < COL_Z:
            g_ref[:, c0 - COL_G:c0 - COL_G + PROJ_CHUNK] = _silu(y).astype(BF16)
        else:
            z_ref[:, c0 - COL_Z:c0 - COL_Z + PROJ_CHUNK] = _silu(y).astype(BF16)

    pad = SSD_CONV // 2

    def conv_slab(j):
        cols = slice(j * LANES, (j + 1) * LANES)
        blk_rows = GRID_W + 2 * CONV_HALO
        for r0 in range(0, TILE, GRID_W):
            rows = slice(r0, r0 + GRID_W)
            blk = xbc_s[r0:r0 + blk_rows, cols]
            below = jnp.where(is_ctx, blk[:CONV_HALO], 0.0)
            above = jnp.where(is_ctx, blk[CONV_HALO + GRID_W:], 0.0)
            blk = jnp.concatenate([below, blk[CONV_HALO:CONV_HALO + GRID_W], above], axis=0)
            acc = jnp.broadcast_to(cb_ref[:, cols], (GRID_W, LANES))
            for k in range(SSD_CONV):
                win = blk if k == pad else pltpu.roll(blk, (pad - k) % blk_rows, axis=0)
                acc = acc + cw_ref[k:k + 1, cols] * win[CONV_HALO:CONV_HALO + GRID_W]
            act = _silu(acc)
            if j < SSD_WIDTH // LANES:
                xs_ref[rows, cols] = act.astype(BF16)
                for dref, h0 in ((dxf_ref, 2 * j), (dxb_ref, SSD_HEADS + 2 * j)):
                    dref[rows, cols] = (_lane_pair(dt[rows, h0:h0 + 1], dt[rows, h0 + 1:h0 + 2]) * act).astype(BF16)
            else:
                bc_ref[rows, j * LANES - SSD_WIDTH:(j + 1) * LANES - SSD_WIDTH] = act.astype(BF16)

    for j in range((SSD_WIDTH + 2 * SSD_BC) // LANES):
        conv_slab(j)
    for c0 in (list(range(COL_F, COL_G, PROJ_CHUNK)) + list(range(COL_Q, COL_F, PROJ_CHUNK))
               + list(range(COL_G, COL_XBC, PROJ_CHUNK))):
        proj_chunk(c0)


def _in_proj(x, ctx, mod3, norm_w, w_in, w_dt, conv_w, conv_b, dt_bias):
    lat = lambda b, t: (b, jnp.maximum(t - 1, 0), 0)
    tok = lambda b, t: (b, t, 0)
    n_tok = N_TILES * TILE

    def out(width, dtype):
        return jax.ShapeDtypeStruct((BATCH, n_tok, width), dtype), pl.BlockSpec((None, TILE, width), tok)

    outs = [out(D_MODEL, BF16), out(2 * D_MODEL, F32), out(D_MODEL, BF16), out(D_MODEL, BF16),
            out(D_MODEL, BF16), out(SSD_WIDTH, BF16), out(SSD_WIDTH, BF16), out(SSD_WIDTH, BF16),
            out(2 * SSD_BC, BF16), out(LANES, F32)]
    return pl.pallas_call(
        _in_proj_kernel,
        grid=(BATCH, N_TILES),
        in_specs=[
            pl.BlockSpec((None, TILE, D_MODEL), lat),
            pl.BlockSpec((None, CTX_LEN, D_MODEL), lambda b, t: (b, 0, 0)),
            pl.BlockSpec((None, 1, N_MOD * D_MODEL), lambda b, t: (jnp.where(t == 0, CTX_ROW, b), 0, 0)),
            _const_spec((1, D_MODEL)),
            _const_spec((D_MODEL, D_IN)),
            _const_spec((D_MODEL, LANES)),
            _const_spec((SSD_CONV, SSD_WIDTH + 2 * SSD_BC)),
            _const_spec((1, SSD_WIDTH + 2 * SSD_BC)),
            _const_spec((1, LANES)),
        ],
        out_specs=[o[1] for o in outs],
        out_shape=[o[0] for o in outs],
        scratch_shapes=[
            pltpu.VMEM((TILE, D_MODEL), BF16),
            pltpu.VMEM((TILE + 2 * CONV_HALO, SSD_WIDTH + 2 * SSD_BC), F32),
        ],
        compiler_params=pltpu.CompilerParams(
            dimension_semantics=("parallel", "arbitrary"), vmem_limit_bytes=VMEM_LIMIT),
        name="in_proj",
    )(x, ctx, mod3, norm_w, w_in, w_dt, conv_w, conv_b, dt_bias)


def _tri(n, reverse):
    i = lax.broadcasted_iota(jnp.int32, (n, n), 0)
    j = lax.broadcasted_iota(jnp.int32, (n, n), 1)
    return (j >= i) if reverse else (j <= i)


def _hgrn_stages(q_ref, f_ref, v_ref, lb_ref, s_ref, qd_s, ki_s, ke_s, sc_s, sp_s, et_s, reverse, emit):
    mask = _tri(HG_CHUNK, reverse)
    tri_b = jnp.where(mask, 1.0, 0.0).astype(BF16)
    i2 = lax.broadcasted_iota(jnp.int32, (HG_CHUNK, LANES), 0)
    j2 = lax.broadcasted_iota(jnp.int32, (HG_CHUNK, LANES), 1) & (HG_CHUNK - 1)
    mask2 = (j2 >= i2) if reverse else (j2 <= i2)
    last = slice(0, 1) if reverse else slice(HG_CHUNK - 1, HG_CHUNK)
    chunk_rows = [slice(c * HG_CHUNK, (c + 1) * HG_CHUNK) for c in range(HG_NC)]
    pair_cols = [slice(p * HG_PAIR, (p + 1) * HG_PAIR) for p in range(HG_HEADS // 2)]
    order = list(reversed(range(HG_NC))) if reverse else list(range(HG_NC))

    def gates(chunks=range(HG_NC)):
        for c in chunks:
            rows = chunk_rows[c]
            for cols in pair_cols:
                lb = lb_ref[:, cols]
                f = lb + (1.0 - lb) * jax.nn.sigmoid(f_ref[rows, cols])
                k = 1.0 - f
                cum = _cumsum_mm(tri_b, jnp.log(f)) * LOG2E
                e_cum = jnp.exp2(cum)
                e_tot = e_cum[last, :]
                k_inv = k / e_cum
                qd_s[rows, cols] = (q_ref[rows, cols] * e_cum).astype(BF16)
                ki_s[rows, cols] = k_inv.astype(BF16)
                ke_s[rows, cols] = (k_inv * e_tot).astype(BF16)
                et_s[c * SUBLANES:c * SUBLANES + 1, cols] = e_tot

    def scores(chunks=range(HG_NC)):
        for c in chunks:
            rows = chunk_rows[c]
            for p, cols in enumerate(pair_cols):
                kp = ki_s[rows, cols]
                sc = _dot_nt(qd_s[rows, cols], _block_diag(kp[:, :HG_F], kp[:, HG_F:]))
                sc_s[rows, p * LANES:(p + 1) * LANES] = jnp.where(mask2, sc, 0.0).astype(BF16)

    def states():
        for h in range(HG_HEADS):
            cols = slice(h * HG_F, (h + 1) * HG_F)
            s = s_ref[h]
            for c in order:
                rows = chunk_rows[c]
                sp_s[c, h] = s.astype(BF16)
                s = s * et_s[c * SUBLANES:c * SUBLANES + 1, cols] + _dot_tn(v_ref[rows, cols], ke_s[rows, cols])
            s_ref[h] = s

    def outputs():
        for c in range(HG_NC):
            rows = chunk_rows[c]
            for p, cols in enumerate(pair_cols):
                vp = v_ref[rows, cols]
                o = _dot(sc_s[rows, p * LANES:(p + 1) * LANES], _block_diag(vp[:, :HG_F], vp[:, HG_F:]))
                o = o + _dot_nt(qd_s[rows, cols], _block_diag(sp_s[c, 2 * p], sp_s[c, 2 * p + 1]))
                emit(rows, p, o)

    return [gates, scores, states, outputs]


def _ssd_stages(dtx_ref, bc_ref, dt_ref, a_ref, s_ref, cum_s, cumt_s, sc_s, m_s, dxe_s, ecum_s, sp_s,
                reverse, dcol, emit):
    mask = _tri(SSD_CHUNK, reverse)
    tri_b = jnp.where(mask, 1.0, 0.0).astype(BF16)
    last = slice(0, 1) if reverse else slice(SSD_CHUNK - 1, SSD_CHUNK)
    chunk_rows = [slice(c * SSD_CHUNK, (c + 1) * SSD_CHUNK) for c in range(SSD_NC)]
    order = list(reversed(range(SSD_NC))) if reverse else list(range(SSD_NC))
    lane = lax.broadcasted_iota(jnp.int32, (SSD_CHUNK, LANES), 1)
    low = lane < SSD_HEADDIM

    def decays():
        for c in range(SSD_NC):
            cum = _cumsum_mm(tri_b, dt_ref[chunk_rows[c], :] * (a_ref[...] * LOG2E))
            cum_s[c] = cum
            cumt_s[c] = cum.T

    def scores():
        for c in range(SSD_NC):
            rows = chunk_rows[c]
            for g in range(SSD_GROUPS):
                bg = bc_ref[rows, g * SSD_STATE:(g + 1) * SSD_STATE]
                cg = bc_ref[rows, SSD_BC + g * SSD_STATE:SSD_BC + (g + 1) * SSD_STATE]
                sc_s[c, g] = _dot_nt(cg, bg)

    def masks(chunks=range(SSD_NC)):
        for c in chunks:
            rows = chunk_rows[c]
            cum = cum_s[c]
            cum_t = cumt_s[c]
            for pair in range(SSD_HEADS // 2):
                rs, tots = [], []
                for h in (2 * pair, 2 * pair + 1):
                    col = dcol + h
                    r = jnp.broadcast_to(cum[:, col:col + 1], (SSD_CHUNK, SSD_CHUNK))
                    row_t = cum_t[col:col + 1, :]
                    decay = jnp.exp2(jnp.where(mask, r - row_t, -jnp.inf))
                    m_s[c, h] = (sc_s[c, h // SSD_HPG] * decay).astype(BF16)
                    rs.append(r)
                    tots.append(row_t[:, last.start:last.stop])
                r2 = jnp.where(low, rs[0], rs[1])
                tot2 = jnp.where(low[0:1, :], tots[0], tots[1])
                cols = slice(pair * LANES, (pair + 1) * LANES)
                ecum_s[rows, cols] = jnp.exp2(r2)
                dxe_s[rows, cols] = (jnp.exp2(tot2 - r2) * dtx_ref[rows, cols].astype(F32)).astype(BF16)

    def states():
        for g in range(SSD_GROUPS):
            grp = slice(g * SSD_GRP_W, (g + 1) * SSD_GRP_W)
            s = s_ref[grp, :]
            for c in order:
                rows = chunk_rows[c]
                sp_s[c, grp, :] = s.astype(BF16)
                ds = _dot_tn(dxe_s[rows, grp], bc_ref[rows, g * SSD_STATE:(g + 1) * SSD_STATE])
                tot_t = cumt_s[c][:, last.start:last.stop]
                dec = jnp.concatenate(
                    [jnp.broadcast_to(jnp.exp2(tot_t[dcol + g * SSD_HPG + hh:dcol + g * SSD_HPG + hh + 1, :]),
                                      (SSD_HEADDIM, SSD_STATE)) for hh in range(SSD_HPG)], axis=0)
                s = s * dec + ds
            s_ref[grp, :] = s

    def outputs():
        for c in range(SSD_NC):
            rows = chunk_rows[c]
            for g in range(SSD_GROUPS):
                grp = slice(g * SSD_GRP_W, (g + 1) * SSD_GRP_W)
                cg = bc_ref[rows, SSD_BC + g * SSD_STATE:SSD_BC + (g + 1) * SSD_STATE]
                y_inter = _dot_nt(cg, sp_s[c, grp, :])
                ys = []
                for pp in range(SSD_HPG // 2):
                    pair = g * (SSD_HPG // 2) + pp
                    cols = slice(pair * LANES, (pair + 1) * LANES)
                    dp = dtx_ref[rows, cols]
                    zero = jnp.zeros_like(dp)
                    rhs = jnp.concatenate([jnp.where(low, dp, zero), jnp.where(low, zero, dp)], axis=0)
                    y_intra = _dot(jnp.concatenate([m_s[c, 2 * pair], m_s[c, 2 * pair + 1]], axis=1), rhs)
                    ys.append(y_intra + ecum_s[rows, cols] * y_inter[:, pp * LANES:(pp + 1) * LANES])
                emit(rows, g, ys)

    return [decays, scores, masks, states, outputs]


def _run_scan_stages(hg, sd):
    gates, hg_scores, hg_states, hg_outputs = hg
    decays, sd_scores, masks, sd_states, sd_outputs = sd
    decays()
    gates([0])
    sd_scores()
    hg_scores([0])
    gates([1])
    masks([0])
    hg_scores([1])
    gates([2])
    masks([1])
    hg_scores([2])
    gates([3])
    hg_scores([3])
    hg_states()
    sd_states()
    hg_outputs()
    sd_outputs()


SCAN_BATCH = 2


def _together(stage_lists):
    return [lambda *args, fs=fs: [f(*args) for f in fs] for fs in zip(*stage_lists)]


_SCAN_SCRATCH = [
    pltpu.VMEM((SCAN_BATCH, HG_HEADS, HG_F, HG_F), F32),
    pltpu.VMEM((SCAN_BATCH, SSD_WIDTH, SSD_STATE), F32),
    pltpu.VMEM((SCAN_BATCH, TILE, D_MODEL), BF16),
    pltpu.VMEM((SCAN_BATCH, TILE, D_MODEL), BF16),
    pltpu.VMEM((SCAN_BATCH, TILE, D_MODEL), BF16),
    pltpu.VMEM((SCAN_BATCH, TILE, HG_HEADS // 2 * LANES), BF16),
    pltpu.VMEM((SCAN_BATCH, HG_NC, HG_HEADS, HG_F, HG_F), BF16),
    pltpu.VMEM((SCAN_BATCH, HG_NC * SUBLANES, D_MODEL), F32),
    pltpu.VMEM((SCAN_BATCH, SSD_NC, SSD_CHUNK, LANES), F32),
    pltpu.VMEM((SCAN_BATCH, SSD_NC, LANES, SSD_CHUNK), F32),
    pltpu.VMEM((SCAN_BATCH, SSD_NC, SSD_GROUPS, SSD_CHUNK, SSD_CHUNK), F32),
    pltpu.VMEM((SCAN_BATCH, SSD_NC, SSD_HEADS, SSD_CHUNK, SSD_CHUNK), BF16),
    pltpu.VMEM((SCAN_BATCH, TILE, SSD_WIDTH), BF16),
    pltpu.VMEM((SCAN_BATCH, TILE, SSD_WIDTH), F32),
    pltpu.VMEM((SCAN_BATCH, SSD_NC, SSD_WIDTH, SSD_STATE), BF16),
]


def _fwd_scan_kernel(q_ref, f_ref, v_ref, dtx_ref, bc_ref, dt_ref, lb_ref, a_ref, o_ref, y_ref, *scratch):
    t = pl.program_id(1)

    @pl.when(t == 0)
    def _():
        for state in scratch[:2]:
            state[...] = jnp.zeros_like(state)

    hgs, sds = [], []
    for i in range(SCAN_BATCH):
        (shg_s, sssd_s, qd_s, ki_s, ke_s, hsc_s, hsp_s, et_s,
         cum_s, cumt_s, ssc_s, m_s, dxe_s, ecum_s, ssp_s) = [s.at[i] for s in scratch]

        def emit_hg(rows, p, o, i=i):
            o_ref[i, rows, p * HG_PAIR:(p + 1) * HG_PAIR] = o.astype(BF16)

        def emit_ssd(rows, g, ys, i=i):
            for n, y in enumerate(ys):
                y_ref[i, rows, g * SSD_GRP_W + n * LANES:g * SSD_GRP_W + (n + 1) * LANES] = y.astype(BF16)

        hgs.append(_hgrn_stages(q_ref.at[i], f_ref.at[i], v_ref.at[i], lb_ref, shg_s, qd_s, ki_s, ke_s,
                                hsc_s, hsp_s, et_s, False, emit_hg))
        sds.append(_ssd_stages(dtx_ref.at[i], bc_ref.at[i], dt_ref.at[i], a_ref, sssd_s, cum_s, cumt_s,
                               ssc_s, m_s, dxe_s, ecum_s, ssp_s, False, 0, emit_ssd))
    _run_scan_stages(_together(hgs), _together(sds))


def _fwd_scan(q, f, v, dtx, bc, dt, lb, a_pad):
    tok = lambda b, t: (b, t, 0)
    lat = lambda b, t: (b, jnp.maximum(t - 1, 0), 0)
    return pl.pallas_call(
        _fwd_scan_kernel,
        grid=(BATCH // SCAN_BATCH, N_TILES),
        in_specs=[
            pl.BlockSpec((SCAN_BATCH, TILE, D_MODEL), tok),
            pl.BlockSpec((SCAN_BATCH, TILE, D_MODEL), tok),
            pl.BlockSpec((SCAN_BATCH, TILE, D_MODEL), tok),
            pl.BlockSpec((SCAN_BATCH, TILE, SSD_WIDTH), tok),
            pl.BlockSpec((SCAN_BATCH, TILE, 2 * SSD_BC), tok),
            pl.BlockSpec((SCAN_BATCH, TILE, LANES), tok),
            pl.BlockSpec((None, 1, D_MODEL), lambda b, t: (0, 0, 0)),
            _const_spec((1, LANES)),
        ],
        out_specs=[pl.BlockSpec((SCAN_BATCH, TILE, D_MODEL), lat), pl.BlockSpec((SCAN_BATCH, TILE, SSD_WIDTH), lat)],
        out_shape=[jax.ShapeDtypeStruct((BATCH, SEQ, D_MODEL), BF16),
                   jax.ShapeDtypeStruct((BATCH, SEQ, SSD_WIDTH), BF16)],
        scratch_shapes=_SCAN_SCRATCH,
        compiler_params=pltpu.CompilerParams(
            dimension_semantics=("parallel", "arbitrary"), vmem_limit_bytes=VMEM_LIMIT),
        name="fwd_scan",
    )(q, f, v, dtx, bc, dt, lb, a_pad)


def _bwd_scan_kernel(q_ref, f_ref, v_ref, dtx_ref, xs_ref, bc_ref, dt_ref, g_ref, z_ref, of_ref, yf_ref,
                     lb_ref, a_ref, hnw_ref, dsk_ref, snw_ref, out_ref, *scratch):
    s = pl.program_id(1)

    @pl.when(s == 0)
    def _():
        for state in scratch[:2]:
            state[...] = jnp.zeros_like(state)

    hgs, sds = [], []
    for i in range(SCAN_BATCH):
        (shg_s, sssd_s, qd_s, ki_s, ke_s, hsc_s, hsp_s, et_s,
         cum_s, cumt_s, ssc_s, m_s, dxe_s, ecum_s, ssp_s) = [r.at[i] for r in scratch]

        def emit_hg(rows, p, o_b, i=i):
            for n in range(2):
                cols = slice(p * HG_PAIR + n * HG_F, p * HG_PAIR + (n + 1) * HG_F)
                o = of_ref[i, rows, cols] + o_b[:, n * HG_F:(n + 1) * HG_F]
                out_ref[i, rows, cols] = (_rms(o, hnw_ref[...]) * g_ref[i, rows, cols]).astype(BF16)

        def emit_ssd(rows, g, ys, i=i):
            us = []
            for n, y_b in enumerate(ys):
                cols = slice(g * SSD_GRP_W + n * LANES, g * SSD_GRP_W + (n + 1) * LANES)
                y = yf_ref[i, rows, cols] + y_b + dsk_ref[:, cols] * xs_ref[i, rows, cols]
                us.append(y * z_ref[i, rows, cols])
            ms = sum(jnp.sum(u * u, axis=-1, keepdims=True) for u in us) / SSD_GRP_W
            inv = lax.rsqrt(ms + EPS)
            for n, u in enumerate(us):
                cols = slice(g * SSD_GRP_W + n * LANES, g * SSD_GRP_W + (n + 1) * LANES)
                out_ref[i, rows, SSD_WIDTH + cols.start:SSD_WIDTH + cols.stop] = (
                    u * inv * snw_ref[:, cols]).astype(BF16)

        hgs.append(_hgrn_stages(q_ref.at[i], f_ref.at[i], v_ref.at[i], lb_ref, shg_s, qd_s, ki_s, ke_s,
                                hsc_s, hsp_s, et_s, True, emit_hg))
        sds.append(_ssd_stages(dtx_ref.at[i], bc_ref.at[i], dt_ref.at[i], a_ref, sssd_s, cum_s, cumt_s,
                               ssc_s, m_s, dxe_s, ecum_s, ssp_s, True, SSD_HEADS, emit_ssd))
    _run_scan_stages(_together(hgs), _together(sds))


def _bwd_scan(q, f, v, dtx, xs, bc, dt, g, z, o_f, y_f, lb, a_pad, hg_norm, d_skip, ssd_norm):
    tok = lambda b, s: (b, jnp.where(s == 0, 0, N_TILES - s), 0)
    fcol = lambda b, s: (b, jnp.where(s == 0, 0, N_TILES - s), 1)
    lat = lambda b, s: (b, jnp.where(s == 0, N_LAT_TILES - 1, N_LAT_TILES - s), 0)
    return pl.pallas_call(
        _bwd_scan_kernel,
        grid=(BATCH // SCAN_BATCH, N_TILES),
        in_specs=[
            pl.BlockSpec((SCAN_BATCH, TILE, D_MODEL), tok),
            pl.BlockSpec((SCAN_BATCH, TILE, D_MODEL), fcol),
            pl.BlockSpec((SCAN_BATCH, TILE, D_MODEL), tok),
            pl.BlockSpec((SCAN_BATCH, TILE, SSD_WIDTH), tok),
            pl.BlockSpec((SCAN_BATCH, TILE, SSD_WIDTH), tok),
            pl.BlockSpec((SCAN_BATCH, TILE, 2 * SSD_BC), tok),
            pl.BlockSpec((SCAN_BATCH, TILE, LANES), tok),
            pl.BlockSpec((SCAN_BATCH, TILE, D_MODEL), tok),
            pl.BlockSpec((SCAN_BATCH, TILE, SSD_WIDTH), tok),
            pl.BlockSpec((SCAN_BATCH, TILE, D_MODEL), lat),
            pl.BlockSpec((SCAN_BATCH, TILE, SSD_WIDTH), lat),
            pl.BlockSpec((None, 1, D_MODEL), lambda b, s: (1, 0, 0)),
            _const_spec((1, LANES)),
            _const_spec((1, HG_F)),
            _const_spec((1, SSD_WIDTH)),
            _const_spec((1, SSD_WIDTH)),
        ],
        out_specs=pl.BlockSpec((SCAN_BATCH, TILE, 2 * D_MODEL), lat),
        out_shape=jax.ShapeDtypeStruct((BATCH, SEQ, 2 * D_MODEL), BF16),
        scratch_shapes=_SCAN_SCRATCH,
        compiler_params=pltpu.CompilerParams(
            dimension_semantics=("parallel", "arbitrary"), vmem_limit_bytes=VMEM_LIMIT),
        name="bwd_scan",
    )(q, f, v, dtx, xs, bc, dt, g, z, o_f, y_f, lb, a_pad, hg_norm, d_skip, ssd_norm)


def _out_ffn_kernel(y_ref, x_ref, mod_ref, wo_ref, nf_ref, wg_ref, wu_ref, wd_ref, fn_ref,
                    out_ref, u_s, h_s, acc_s):
    g1 = mod_ref[:, 2 * D_MODEL:3 * D_MODEL]
    sh2 = mod_ref[:, 3 * D_MODEL:4 * D_MODEL]
    sc2 = mod_ref[:, 4 * D_MODEL:5 * D_MODEL]
    g2 = mod_ref[:, 5 * D_MODEL:6 * D_MODEL]
    halves = [slice(i * TILE, (i + 1) * TILE) for i in range(FFN_TILE // TILE)]
    n_chunks = D_FF // FFN_CHUNK

    for r in halves:
        h = x_ref[r, :] + g1 * _dot(y_ref[r, :], wo_ref[...])
        h_s[r, :] = h
        u_s[r, :] = (_rms(h, nf_ref[...]) * (1.0 + sc2) + sh2).astype(BF16)

    for r in halves:
        def gate_up(c):
            cols = slice(c * FFN_CHUNK, (c + 1) * FFN_CHUNK)
            return _dot(u_s[r, :], wg_ref[:, cols]), _dot(u_s[r, :], wu_ref[:, cols])

        cur = gate_up(0)
        for c in range(n_chunks):
            nxt = gate_up(c + 1) if c + 1 < n_chunks else None
            part = _dot((_silu(cur[0]) * cur[1]).astype(BF16), wd_ref[c * FFN_CHUNK:(c + 1) * FFN_CHUNK, :])
            if c == 0:
                acc_s[r, :] = part
            else:
                acc_s[r, :] += part
            cur = nxt
        out_ref[r, :] = _rms(h_s[r, :] + g2 * acc_s[r, :], fn_ref[...])


def _out_ffn(ymix, x, mod3, w_out, norm_ffn, w_gate, w_up, w_down, final_norm):
    tok = lambda b, t: (b, t, 0)
    return pl.pallas_call(
        _out_ffn_kernel,
        grid=(BATCH, SEQ // FFN_TILE),
        in_specs=[
            pl.BlockSpec((None, FFN_TILE, 2 * D_MODEL), tok),
            pl.BlockSpec((None, FFN_TILE, D_MODEL), tok),
            pl.BlockSpec((None, 1, N_MOD * D_MODEL), lambda b, t: (b, 0, 0)),
            _const_spec((2 * D_MODEL, D_MODEL)),
            _const_spec((1, D_MODEL)),
            _const_spec((D_MODEL, D_FF)),
            _const_spec((D_MODEL, D_FF)),
            _const_spec((D_FF, D_MODEL)),
            _const_spec((1, D_MODEL)),
        ],
        out_specs=pl.BlockSpec((None, FFN_TILE, D_MODEL), tok),
        out_shape=jax.ShapeDtypeStruct((BATCH, SEQ, D_MODEL), F32),
        scratch_shapes=[pltpu.VMEM((FFN_TILE, D_MODEL), BF16), pltpu.VMEM((FFN_TILE, D_MODEL), F32),
                        pltpu.VMEM((FFN_TILE, D_MODEL), F32)],
        compiler_params=pltpu.CompilerParams(
            dimension_semantics=("parallel", "arbitrary"), vmem_limit_bytes=VMEM_LIMIT),
        name="out_ffn",
    )(ymix, x, mod3, w_out, norm_ffn, w_gate, w_up, w_down, final_norm)


def kernel(x, c, ctx, c_ctx, w_ada, b_ada, norm_mix, w_in, conv_w, conv_b, ssd_a_log, ssd_dt_bias,
           ssd_d, ssd_norm, hgrn_lb_raw, hgrn_norm, w_out, norm_ffn, w_gate, w_up, w_down, final_norm):
    assert x.shape == (BATCH, SEQ, D_MODEL) and ctx.shape == (BATCH, CTX_LEN, D_MODEL)
    assert w_ada.shape == (1, D_MODEL, N_MOD * D_MODEL) and w_in.shape == (1, D_MODEL, D_IN)

    c_all = jnp.concatenate([c, c_ctx[None, :], jnp.zeros((MOD_ROWS - BATCH - 1, D_MODEL), F32)], axis=0)
    w_in_b = w_in[0].astype(BF16)
    w_dt = jnp.pad(w_in_b[:, COL_DT:], ((0, 0), (0, LANES - 2 * SSD_HEADS)))
    dt_bias = jnp.pad(ssd_dt_bias[0].reshape(1, 2 * SSD_HEADS), ((0, 0), (0, LANES - 2 * SSD_HEADS)))
    a_pad = jnp.pad(-jnp.exp(ssd_a_log[0].astype(F32)).reshape(1, 2 * SSD_HEADS),
                    ((0, 0), (0, LANES - 2 * SSD_HEADS)))
    lbs = jnp.cumsum(jax.nn.softmax(hgrn_lb_raw.astype(F32), axis=0), axis=0)[0][:, None, :]
    d_skip = jnp.repeat(ssd_d[0], SSD_HEADDIM)[None, :]

    mod = _ada(c_all, w_ada, b_ada)
    mod3 = mod.reshape(MOD_ROWS, 1, N_MOD * D_MODEL)
    q, f, v, g, z, xs, dtx_f, dtx_b, bc, dt = _in_proj(
        x, ctx, mod3, norm_mix, w_in_b, w_dt, conv_w[0], conv_b, dt_bias)
    o_f, y_f = _fwd_scan(q, f, v, dtx_f, bc, dt, lbs, a_pad)
    ymix = _bwd_scan(q, f, v, dtx_b, xs, bc, dt, g, z, o_f, y_f, lbs, a_pad,
                     hgrn_norm, d_skip, ssd_norm)
    return _out_ffn(ymix, x, mod3, w_out[0].astype(BF16), norm_ffn, w_gate[0].astype(BF16),
                    w_up[0].astype(BF16), w_down[0].astype(BF16), final_norm[None, :])
```
